```python
import math
import jax, jax.numpy as jnp
from jax import lax
import numpy as np

D_MODEL = 1024
BATCH = 1
SEQ = 16384
DEPTH = 4

CHUNK = 64
N_A_LAYERS = DEPTH // 2
N_B_LAYERS = DEPTH - N_A_LAYERS
SSM_GROUP = 16
SSM_GROUPS = D_MODEL // SSM_GROUP
SSM_STATE = 64
DT_MIN = 1e-3
DT_MAX = 1e-1
N_HEADS = 16
HEAD_DIM = D_MODEL // N_HEADS
Q_BLOCK = 2 * CHUNK
D_FF = ((8 * D_MODEL + 3 * 256 - 1) // (3 * 256)) * 256
EPS = 1e-6

kernel_name = "yoco_s5_stickbreaking_trunk"


def rms_norm(x, g):
    xf = x.astype(jnp.float32)
    y = xf * lax.rsqrt(jnp.mean(xf * xf, axis=-1, keepdims=True) + EPS)
    return (y * g.astype(jnp.float32)).astype(x.dtype)


def swiglu(x, w_in, w_out):
    gate, up = jnp.split(x @ w_in, 2, axis=-1)
    return (jax.nn.silu(gate) * up) @ w_out


def s5_mixer(u, log_dt, a_re, a_im, b_re, b_im, c_re, c_im, d_skip, w_glu, b_glu):
    f32 = jnp.float32
    bsz, seq_len, _ = u.shape
    uf = u.astype(f32).reshape(bsz, seq_len, SSM_GROUPS, SSM_GROUP)
    lam = lax.complex(a_re.astype(f32), a_im.astype(f32))
    dt = jnp.exp(log_dt.astype(f32))[:, None]
    lam_bar = jnp.exp(lam * dt)
    b_mat = lax.complex(b_re.astype(f32), b_im.astype(f32))
    b_bar = ((lam_bar - 1.0) / lam)[..., None] * b_mat
    bu = jnp.einsum('blgc,gpc->blgp', uf.astype(jnp.complex64), b_bar)
    a_elems = jnp.broadcast_to(lam_bar, bu.shape)

    def combine(left, right):
        a_l, b_l = left
        a_r, b_r = right
        return a_r * a_l, a_r * b_l + b_r

    _, states = lax.associative_scan(combine, (a_elems, bu), axis=1)
    c_mat = lax.complex(c_re.astype(f32), c_im.astype(f32))
    y = jnp.einsum('blgp,gcp->blgc', states, c_mat).real
    y = y + d_skip.astype(f32) * uf
    y = jax.nn.gelu(y.reshape(bsz, seq_len, D_MODEL)).astype(u.dtype)
    val, gate = jnp.split(y @ w_glu + b_glu, 2, axis=-1)
    return val * jax.nn.sigmoid(gate)


def stick_breaking_attention(q, k, v):
    bsz, n_h, seq_len, dh = q.shape
    nblk = seq_len // Q_BLOCK

    def to_blocks(a):
        return a.reshape(bsz, n_h, nblk, Q_BLOCK, dh).transpose(2, 0, 1, 3, 4)

    qb = to_blocks(q * (dh ** -0.5))
    kb = to_blocks(k)
    vb = to_blocks(v)
    pos = jnp.arange(Q_BLOCK)
    diag_mask = pos[:, None] > pos[None, :]

    def one_query_block(args):
        i, q_i = args

        def body(step, carry):
            out, log_rest = carry
            j = i - step
            k_j = kb[j]
            v_j = vb[j]
            z = jnp.einsum('bhqd,bhkd->bhqk', q_i, k_j)
            mask = jnp.where(j == i, diag_mask, True)
            log_beta = jax.nn.log_sigmoid(z)
            log_1m = jnp.where(mask, jax.nn.log_sigmoid(-z), 0.0)
            later = lax.cumsum(log_1m, axis=3, reverse=True) - log_1m
            w = jnp.where(mask, jnp.exp(log_beta + later + log_rest[..., None]), 0.0)
            out = out + jnp.einsum('bhqk,bhkd->bhqd', w, v_j)
            log_rest = log_rest + jnp.sum(log_1m, axis=3)
            return out, log_rest

        init = (jnp.zeros_like(q_i), jnp.zeros(q_i.shape[:-1], jnp.float32))
        out, _ = lax.fori_loop(0, i + 1, body, init)
        return out

    out = lax.map(one_query_block, (jnp.arange(nblk), qb))
    return out.transpose(1, 2, 0, 3, 4).reshape(bsz, n_h, seq_len, dh)


def split_heads(t):
    bsz, seq_len, _ = t.shape
    return t.reshape(bsz, seq_len, N_HEADS, HEAD_DIM).transpose(0, 2, 1, 3).astype(jnp.float32)


def setup_inputs(seed: int = 0) -> dict:
    key = jax.random.key(seed)
    ks = jax.random.split(key, 24)
    f32 = jnp.float32

    def nrm(k, shape, scale):
        return jax.random.normal(k, shape, f32) * scale

    def gains(k, shape):
        return 1.0 + 0.02 * jax.random.normal(k, shape, f32)

    n_idx = jnp.arange(SSM_STATE, dtype=f32)
    a_im0 = jnp.broadcast_to(math.pi * n_idx, (N_A_LAYERS, SSM_GROUPS, SSM_STATE))
    return {
        'x': jax.random.normal(ks[0], (BATCH, SEQ, D_MODEL), f32),
        'norm_mix_pre': gains(ks[1], (DEPTH, D_MODEL)),
        'norm_mix_post': gains(ks[2], (DEPTH, D_MODEL)),
        'norm_ffn_pre': gains(ks[3], (DEPTH, D_MODEL)),
        'norm_ffn_post': gains(ks[4], (DEPTH, D_MODEL)),
        'w_ffn_in': nrm(ks[5], (DEPTH, D_MODEL, 2 * D_FF), D_MODEL ** -0.5),
        'w_ffn_out': nrm(ks[6], (DEPTH, D_FF, D_MODEL), D_FF ** -0.5),
        's5_log_dt': jax.random.uniform(ks[7], (N_A_LAYERS, SSM_GROUPS), f32,
                                        math.log(DT_MIN), math.log(DT_MAX)),
        's5_a_re': -0.5 + 0.01 * jax.random.normal(ks[8], (N_A_LAYERS, SSM_GROUPS, SSM_STATE), f32),
        's5_a_im': a_im0 + 0.01 * jax.random.normal(ks[9], (N_A_LAYERS, SSM_GROUPS, SSM_STATE), f32),
        's5_b_re': nrm(ks[10], (N_A_LAYERS, SSM_GROUPS, SSM_STATE, SSM_GROUP), (2 * SSM_GROUP) ** -0.5),
        's5_b_im': nrm(ks[11], (N_A_LAYERS, SSM_GROUPS, SSM_STATE, SSM_GROUP), (2 * SSM_GROUP) ** -0.5),
        's5_c_re': nrm(ks[12], (N_A_LAYERS, SSM_GROUPS, SSM_GROUP, SSM_STATE), (2 * SSM_STATE) ** -0.5),
        's5_c_im': nrm(ks[13], (N_A_LAYERS, SSM_GROUPS, SSM_GROUP, SSM_STATE), (2 * SSM_STATE) ** -0.5),
        's5_d': nrm(ks[14], (N_A_LAYERS, SSM_GROUPS, SSM_GROUP), 1.0),
        's5_w_glu': nrm(ks[15], (N_A_LAYERS, D_MODEL, 2 * D_MODEL), D_MODEL ** -0.5),
        's5_b_glu': nrm(ks[16], (N_A_LAYERS, 2 * D_MODEL), 0.01),
        'kv_norm': gains(ks[17], (D_MODEL,)),
        'w_k': nrm(ks[18], (D_MODEL, D_MODEL), D_MODEL ** -0.5),
        'w_v': nrm(ks[19], (D_MODEL, D_MODEL), D_MODEL ** -0.5),
        'w_q': nrm(ks[20], (N_B_LAYERS, D_MODEL, D_MODEL), D_MODEL ** -0.5),
        'w_o': nrm(ks[21], (N_B_LAYERS, D_MODEL, D_MODEL), D_MODEL ** -0.5),
    }


def reference(x, norm_mix_pre, norm_mix_post, norm_ffn_pre, norm_ffn_post, w_ffn_in, w_ffn_out,
              s5_log_dt, s5_a_re, s5_a_im, s5_b_re, s5_b_im, s5_c_re, s5_c_im, s5_d,
              s5_w_glu, s5_b_glu, kv_norm, w_k, w_v, w_q, w_o):
    bsz, seq_len, _ = x.shape
    h = x
    k_shared = None
    v_shared = None
    for layer in range(DEPTH):
        u = rms_norm(h, norm_mix_pre[layer])
        if layer < N_A_LAYERS:
            mix = s5_mixer(u, s5_log_dt[layer], s5_a_re[layer], s5_a_im[layer],
                           s5_b_re[layer], s5_b_im[layer], s5_c_re[layer], s5_c_im[layer],
                           s5_d[layer], s5_w_glu[layer], s5_b_glu[layer])
        else:
            if layer == N_A_LAYERS:
                hk = rms_norm(h, kv_norm)
                k_shared = split_heads(hk @ w_k)
                v_shared = split_heads(hk @ w_v)
            b = layer - N_A_LAYERS
            q = split_heads(u @ w_q[b])
            o = stick_breaking_attention(q, k_shared, v_shared)
            o = o.transpose(0, 2, 1, 3).reshape(bsz, seq_len, D_MODEL).astype(h.dtype)
            mix = o @ w_o[b]
        h = h + rms_norm(mix, norm_mix_post[layer])
        f = swiglu(rms_norm(h, norm_ffn_pre[layer]), w_ffn_in[layer], w_ffn_out[layer])
        h = h + rms_norm(f, norm_ffn_post[layer])
    return h
```

```python
import functools
import math

import jax
import jax.numpy as jnp
from jax import lax
from jax.experimental import pallas as pl
from jax.experimental.pallas import tpu as pltpu

F32 = jnp.float32
BF16 = jnp.bfloat16

D_MODEL = 1024
DEPTH = 4
N_A_LAYERS = DEPTH // 2
SSM_GROUP = 16
SSM_GROUPS = D_MODEL // SSM_GROUP
SSM_STATE = 64
N_HEADS = 16
HEAD_DIM = D_MODEL // N_HEADS
D_FF = 2816
EPS = 1e-6

V7X_LANES = 128
V7X_SUBLANES = 8
V7X_MXU_DIM = 256
V7X_VMEM_BYTES = 64 * 1024 * 1024

S5_KTILES = D_MODEL // V7X_MXU_DIM
S5_TILE_GROUPS = V7X_MXU_DIM // SSM_GROUP
S5_TILE_STATES = S5_TILE_GROUPS * SSM_STATE
S5_SEG = 64
S5_ROWS = V7X_SUBLANES * S5_SEG
S5_COLS = 512

FFN_ROWS = 512
PROJ_ROWS = 512

ATT_KEYS = 128
ATT_QBLOCKS = 4
ATT_ROWS = ATT_QBLOCKS * ATT_KEYS
HEADS_PER_STEP = V7X_LANES // HEAD_DIM
ATT_LOG_CUTOFF = -104.0


def _vmem_limit(nbytes):
    return int(min(nbytes, V7X_VMEM_BYTES - 6 * 1024 * 1024))


def _rms(x, g):
    return x * lax.rsqrt(jnp.mean(x * x, axis=-1, keepdims=True) + EPS) * g


def _dot(a, b):
    return jnp.dot(a, b, preferred_element_type=F32)


def _const_spec(shape):
    zeros = (0,) * len(shape)
    return pl.BlockSpec(shape, lambda *_: zeros, pipeline_mode=pl.Buffered(1))


def _s5_kernel(h_ref, gpre_ref, gpost_ref, wb_ref, wc_ref, lre_ref, lim_ref, mre_ref, mim_ref,
               pwre_ref, pwim_ref, d_ref, wglu_ref, bglu_ref, o_ref,
               xre_ref, xim_ref, cre_ref, cim_ref, y_ref):
    @pl.when(pl.program_id(0) == 0)
    def _():
        cre_ref[...] = jnp.zeros_like(cre_ref)
        cim_ref[...] = jnp.zeros_like(cim_ref)

    h = h_ref[...]
    u = _rms(h, gpre_ref[...])
    u_bf = u.astype(BF16)
    zeros = jnp.zeros((V7X_SUBLANES, S5_COLS), F32)
    row = lax.broadcasted_iota(jnp.int32, (V7X_SUBLANES, S5_COLS), 0)

    for kt in range(S5_KTILES):
        bu = _dot(u_bf[:, kt * V7X_MXU_DIM:(kt + 1) * V7X_MXU_DIM], wb_ref[kt])
        xre_ref[...] = bu[:, :S5_TILE_STATES]
        xim_ref[...] = bu[:, S5_TILE_STATES:]

        for cg in range(S5_TILE_STATES // S5_COLS):
            cs = slice(cg * S5_COLS, (cg + 1) * S5_COLS)
            lre = jnp.broadcast_to(lre_ref[kt, :, cs], (V7X_SUBLANES, S5_COLS))
            lim = jnp.broadcast_to(lim_ref[kt, :, cs], (V7X_SUBLANES, S5_COLS))

            def local_scan(i, carry):
                xr, xi = carry
                r0 = pl.multiple_of(i * V7X_SUBLANES, V7X_SUBLANES)
                nr = lre * xr - lim * xi + xre_ref[pl.ds(r0, V7X_SUBLANES), cs]
                ni = lre * xi + lim * xr + xim_ref[pl.ds(r0, V7X_SUBLANES), cs]
                xre_ref[pl.ds(r0, V7X_SUBLANES), cs] = nr
                xim_ref[pl.ds(r0, V7X_SUBLANES), cs] = ni
                return nr, ni

            er, ei = lax.fori_loop(0, S5_SEG, local_scan, (zeros, zeros), unroll=4)

            mre = mre_ref[kt, :, cs]
            mim = mim_ref[kt, :, cs]
            cur_r = cre_ref[kt, :, cs]
            cur_i = cim_ref[kt, :, cs]
            xin_r = zeros
            xin_i = zeros
            for r in range(V7X_SUBLANES):
                xin_r = jnp.where(row == r, cur_r, xin_r)
                xin_i = jnp.where(row == r, cur_i, xin_i)
                nxt_r = mre * cur_r - mim * cur_i + er[r:r + 1, :]
                nxt_i = mre * cur_i + mim * cur_r + ei[r:r + 1, :]
                cur_r, cur_i = nxt_r, nxt_i
            cre_ref[kt, :, cs] = cur_r
            cim_ref[kt, :, cs] = cur_i

            def add_incoming(i, c):
                r0 = pl.multiple_of(i * V7X_SUBLANES, V7X_SUBLANES)
                pr = pwre_ref[kt, pl.ds(i, 1), cs]
                pi = pwim_ref[kt, pl.ds(i, 1), cs]
                xre_ref[pl.ds(r0, V7X_SUBLANES), cs] += pr * xin_r - pi * xin_i
                xim_ref[pl.ds(r0, V7X_SUBLANES), cs] += pr * xin_i + pi * xin_r
                return c

            lax.fori_loop(0, S5_SEG, add_incoming, 0, unroll=4)

        y_ref[:, kt * V7X_MXU_DIM:(kt + 1) * V7X_MXU_DIM] = (
            _dot(xre_ref[...].astype(BF16), wc_ref[kt, :S5_TILE_STATES, :])
            + _dot(xim_ref[...].astype(BF16), wc_ref[kt, S5_TILE_STATES:, :]))

    y = y_ref[...] + d_ref[...] * u
    a = jax.nn.gelu(y).astype(BF16)
    z = _dot(a, wglu_ref[...]) + bglu_ref[...]
    mix = z[:, :D_MODEL] * jax.nn.sigmoid(z[:, D_MODEL:])
    o_ref[...] = h + _rms(mix, gpost_ref[...])


def _s5_params(log_dt, a_re, a_im, b_re, b_im, c_re, c_im):
    dt = jnp.exp(log_dt.astype(F32))[:, None]
    a_re = a_re.astype(F32)
    a_im = a_im.astype(F32)
    mag = jnp.exp(a_re * dt)
    lam_re = mag * jnp.cos(a_im * dt)
    lam_im = mag * jnp.sin(a_im * dt)
    den = a_re * a_re + a_im * a_im
    nr = lam_re - 1.0
    coef_re = (nr * a_re + lam_im * a_im) / den
    coef_im = (lam_im * a_re - nr * a_im) / den
    bb_re = coef_re[..., None] * b_re - coef_im[..., None] * b_im
    bb_im = coef_re[..., None] * b_im + coef_im[..., None] * b_re

    eye = jnp.eye(S5_TILE_GROUPS, dtype=F32)

    def in_proj(b):
        b = b.reshape(S5_KTILES, S5_TILE_GROUPS, SSM_STATE, SSM_GROUP)
        w = jnp.einsum('kgpc,gh->kgchp', b, eye)
        return w.reshape(S5_KTILES, V7X_MXU_DIM, S5_TILE_STATES)

    def out_proj(c):
        c = c.astype(F32).reshape(S5_KTILES, S5_TILE_GROUPS, SSM_GROUP, SSM_STATE)
        w = jnp.einsum('kgcp,gh->kgphc', c, eye)
        return w.reshape(S5_KTILES, S5_TILE_STATES, V7X_MXU_DIM)

    wb = jnp.concatenate([in_proj(bb_re), in_proj(bb_im)], axis=2).astype(BF16)
    wc = jnp.concatenate([out_proj(c_re), -out_proj(c_im)], axis=1).astype(BF16)

    def tile(v):
        return v.reshape(S5_KTILES, 1, S5_TILE_STATES)

    def power(n):
        m = jnp.exp(n * a_re * dt)
        return m * jnp.cos(n * a_im * dt), m * jnp.sin(n * a_im * dt)

    m_re, m_im = power(float(S5_SEG))
    steps = jnp.arange(1, S5_SEG + 1, dtype=F32)[:, None, None]
    pm = jnp.exp(steps * (a_re * dt))
    pw_re = pm * jnp.cos(steps * (a_im * dt))
    pw_im = pm * jnp.sin(steps * (a_im * dt))

    def tile_steps(v):
        return v.reshape(S5_SEG, S5_KTILES, S5_TILE_STATES).transpose(1, 0, 2)

    return (wb, wc, tile(lam_re), tile(lam_im), tile(m_re), tile(m_im),
            tile_steps(pw_re), tile_steps(pw_im))


def _s5_layer(h, g_pre, g_post, params, d_skip, w_glu, b_glu):
    seq = h.shape[0]
    wb, wc, lre, lim, mre, mim, pwre, pwim = params
    row_spec = pl.BlockSpec((S5_ROWS, D_MODEL), lambda i: (i, 0))
    args = (h, g_pre.reshape(1, D_MODEL), g_post.reshape(1, D_MODEL), wb, wc, lre, lim, mre, mim,
            pwre, pwim, d_skip.reshape(1, D_MODEL).astype(F32), w_glu.astype(BF16),
            b_glu.reshape(1, 2 * D_MODEL).astype(F32))
    in_specs = [row_spec] + [_const_spec(a.shape) for a in args[1:]]
    return pl.pallas_call(
        _s5_kernel,
        name="s5_mixer",
        grid=(seq // S5_ROWS,),
        in_specs=in_specs,
        out_specs=row_spec,
        out_shape=jax.ShapeDtypeStruct((seq, D_MODEL), F32),
        scratch_shapes=[
            pltpu.VMEM((S5_ROWS, S5_TILE_STATES), F32),
            pltpu.VMEM((S5_ROWS, S5_TILE_STATES), F32),
            pltpu.VMEM((S5_KTILES, 1, S5_TILE_STATES), F32),
            pltpu.VMEM((S5_KTILES, 1, S5_TILE_STATES), F32),
            pltpu.VMEM((S5_ROWS, D_MODEL), F32),
        ],
        compiler_params=pltpu.CompilerParams(
            dimension_semantics=("arbitrary",), vmem_limit_bytes=_vmem_limit(56 * 1024 * 1024)),
    )(*args)


def _ffn_body(h, gpre_ref, gpost_ref, win_ref, wout_ref, o_ref):
    xn = _rms(h, gpre_ref[...]).astype(BF16)
    hd = _dot(xn, win_ref[...])
    a = (jax.nn.silu(hd[:, :D_FF]) * hd[:, D_FF:]).astype(BF16)
    f = _dot(a, wout_ref[...])
    o_ref[...] = h + _rms(f, gpost_ref[...])


def _ffn_kernel(h_ref, gpre_ref, gpost_ref, win_ref, wout_ref, o_ref):
    _ffn_body(h_ref[...], gpre_ref, gpost_ref, win_ref, wout_ref, o_ref)


def _oproj_ffn_kernel(h_ref, att_ref, wo_ref, gmix_ref, gpre_ref, gpost_ref, win_ref, wout_ref, o_ref):
    h = h_ref[...] + _rms(_dot(att_ref[...], wo_ref[...]), gmix_ref[...])
    _ffn_body(h, gpre_ref, gpost_ref, win_ref, wout_ref, o_ref)


def _ffn_layer(h, g_pre, g_post, w_in, w_out, att=None, w_o=None, g_mix=None):
    seq = h.shape[0]
    row_spec = pl.BlockSpec((FFN_ROWS, D_MODEL), lambda i: (i, 0))
    tail = (g_pre.reshape(1, D_MODEL), g_post.reshape(1, D_MODEL), w_in.astype(BF16), w_out.astype(BF16))
    if att is None:
        kern, args, specs = _ffn_kernel, (h,) + tail, [row_spec]
    else:
        head = (att, w_o.astype(BF16), g_mix.reshape(1, D_MODEL))
        kern, args = _oproj_ffn_kernel, (h,) + head + tail
        specs = [row_spec, row_spec, _const_spec(head[1].shape), _const_spec(head[2].shape)]
    specs = specs + [_const_spec(a.shape) for a in tail]
    return pl.pallas_call(
        kern,
        name="ffn",
        grid=(seq // FFN_ROWS,),
        in_specs=specs,
        out_specs=row_spec,
        out_shape=jax.ShapeDtypeStruct((seq, D_MODEL), F32),
        compiler_params=pltpu.CompilerParams(
            dimension_semantics=("arbitrary",), vmem_limit_bytes=_vmem_limit(56 * 1024 * 1024)),
    )(*args)


def _q_of(h, gq_ref, wq_ref):
    u = _rms(h, gq_ref[...]).astype(BF16)
    return (_dot(u, wq_ref[...]) * (HEAD_DIM ** -0.5)).astype(BF16)


def _qkv_kernel(h_ref, gq_ref, wq_ref, gkv_ref, wkt_ref, wv_ref, q_ref, kt_ref, v_ref):
    h = h_ref[...]
    q_ref[...] = _q_of(h, gq_ref, wq_ref)
    hk = _rms(h, gkv_ref[...]).astype(BF16)
    kt_ref[...] = lax.dot_general(wkt_ref[...], hk, (((1,), (1,)), ((), ())),
                                  preferred_element_type=F32).astype(BF16)
    v_ref[...] = _dot(hk, wv_ref[...]).astype(BF16)


def _q_kernel(h_ref, gq_ref, wq_ref, q_ref):
    q_ref[...] = _q_of(h_ref[...], gq_ref, wq_ref)


def _qkv_proj(h, g_q, w_q, g_kv, w_k, w_v):
    seq = h.shape[0]
    row_spec = pl.BlockSpec((PROJ_ROWS, D_MODEL), lambda i: (i, 0))
    col_spec = pl.BlockSpec((D_MODEL, PROJ_ROWS), lambda i: (0, i))
    args = (h, g_q.reshape(1, D_MODEL), w_q.astype(BF16), g_kv.reshape(1, D_MODEL),
            w_k.T.astype(BF16), w_v.astype(BF16))
    return pl.pallas_call(
        _qkv_kernel,
        name="qkv_proj",
        grid=(seq // PROJ_ROWS,),
        in_specs=[row_spec] + [_const_spec(a.shape) for a in args[1:]],
        out_specs=[row_spec, col_spec, row_spec],
        out_shape=[jax.ShapeDtypeStruct((seq, D_MODEL), BF16),
                   jax.ShapeDtypeStruct((D_MODEL, seq), BF16),
                   jax.ShapeDtypeStruct((seq, D_MODEL), BF16)],
        compiler_params=pltpu.CompilerParams(
            dimension_semantics=("arbitrary",), vmem_limit_bytes=_vmem_limit(40 * 1024 * 1024)),
    )(*args)


def _q_proj(h, g_q, w_q):
    seq = h.shape[0]
    row_spec = pl.BlockSpec((PROJ_ROWS, D_MODEL), lambda i: (i, 0))
    args = (h, g_q.reshape(1, D_MODEL), w_q.astype(BF16))
    return pl.pallas_call(
        _q_kernel,
        name="q_proj",
        grid=(seq // PROJ_ROWS,),
        in_specs=[row_spec] + [_const_spec(a.shape) for a in args[1:]],
        out_specs=row_spec,
        out_shape=jax.ShapeDtypeStruct((seq, D_MODEL), BF16),
        compiler_params=pltpu.CompilerParams(
            dimension_semantics=("arbitrary",), vmem_limit_bytes=_vmem_limit(40 * 1024 * 1024)),
    )(*args)


def _attn_kernel(q_ref, kt_ref, v_ref, o_ref, acc_ref, lr_ref):
    tile = pl.program_id(1)
    shape = (ATT_KEYS, ATT_KEYS)
    row = lax.broadcasted_iota(jnp.int32, shape, 0)
    lane = lax.broadcasted_iota(jnp.int32, shape, 1)
    head_lanes = [(lane >= hh * HEAD_DIM) & (lane < (hh + 1) * HEAD_DIM) for hh in range(HEADS_PER_STEP)]
    causal = row > lane
    trow = lax.broadcasted_iota(jnp.int32, (ATT_KEYS, 2 * ATT_KEYS), 0)
    tcol = lax.broadcasted_iota(jnp.int32, (ATT_KEYS, 2 * ATT_KEYS), 1)
    tri_ones = jnp.where((trow >= tcol) | (tcol >= ATT_KEYS), 1.0, 0.0).astype(BF16)

    def chain(qb, hh, kv_block, keep, first):
        q = q_ref[qb * ATT_KEYS:(qb + 1) * ATT_KEYS, :]
        qm = jnp.where(head_lanes[hh], q, jnp.zeros_like(q))
        k0 = pl.multiple_of(kv_block * ATT_KEYS, ATT_KEYS)
        z = _dot(qm, kt_ref[:, pl.ds(k0, ATT_KEYS)])
        sp = jnp.maximum(z, 0.0) + jnp.log(1.0 + jnp.exp(-jnp.abs(z)))
        sp = jnp.where(keep, sp, 0.0)
        sp_hi = sp.astype(BF16)
        sp_lo = (sp - sp_hi.astype(F32)).astype(BF16)
        sums = _dot(sp_hi, tri_ones) + _dot(sp_lo, tri_ones)
        suffix = sums[:, :ATT_KEYS]
        total = sums[:, ATT_KEYS:]
        if first:
            logw = z - suffix
        else:
            log_rest = lr_ref[qb, hh]
            logw = z - suffix + log_rest
        w = jnp.where(keep, jnp.exp(logw), 0.0).astype(BF16)
        pv = _dot(w, v_ref[pl.ds(k0, ATT_KEYS), :])
        if first:
            acc_ref[qb, hh] = pv
            lr_ref[qb, hh] = -total
        else:
            acc_ref[qb, hh] += pv
            lr_ref[qb, hh] = log_rest - total

    first_block = tile * ATT_QBLOCKS
    for qb in range(ATT_QBLOCKS):
        for hh in range(HEADS_PER_STEP):
            chain(qb, hh, first_block + qb, causal, True)

    def more(carry):
        step, top = carry
        return (step < first_block + ATT_QBLOCKS) & (top > ATT_LOG_CUTOFF)

    def walk(carry):
        step, _ = carry
        for qb in range(ATT_QBLOCKS):
            kv_block = first_block + qb - step
            for hh in range(HEADS_PER_STEP):
                chain(qb, hh, jnp.maximum(kv_block, 0), kv_block >= 0, False)
        return step + 1, jnp.max(lr_ref[...])

    lax.while_loop(more, walk, (jnp.int32(1), jnp.max(lr_ref[...])))

    for qb in range(ATT_QBLOCKS):
        out = jnp.where(head_lanes[0], acc_ref[qb, 0], acc_ref[qb, 1])
        o_ref[qb * ATT_KEYS:(qb + 1) * ATT_KEYS, :] = out.astype(BF16)


def _attention(q, kt, v):
    seq = q.shape[0]
    state = pltpu.VMEM((ATT_QBLOCKS, HEADS_PER_STEP, ATT_KEYS, ATT_KEYS), F32)
    return pl.pallas_call(
        _attn_kernel,
        name="stickbreaking_attention",
        grid=(N_HEADS // HEADS_PER_STEP, seq // ATT_ROWS),
        in_specs=[pl.BlockSpec((ATT_ROWS, V7X_LANES), lambda p, i: (i, p)),
                  pl.BlockSpec((V7X_LANES, seq), lambda p, i: (p, 0)),
                  pl.BlockSpec((seq, V7X_LANES), lambda p, i: (0, p))],
        out_specs=pl.BlockSpec((ATT_ROWS, V7X_LANES), lambda p, i: (i, p)),
        out_shape=jax.ShapeDtypeStruct((seq, D_MODEL), BF16),
        scratch_shapes=[state, state],
        compiler_params=pltpu.CompilerParams(
            dimension_semantics=("arbitrary", "arbitrary"), vmem_limit_bytes=_vmem_limit(40 * 1024 * 1024)),
    )(q, kt, v)


def kernel(x, norm_mix_pre, norm_mix_post, norm_ffn_pre, norm_ffn_post, w_ffn_in, w_ffn_out, s5_log_dt, s5_a_re, s5_a_im, s5_b_re, s5_b_im, s5_c_re, s5_c_im, s5_d, s5_w_glu, s5_b_glu, kv_norm, w_k, w_v, w_q, w_o):
    bsz, seq, _ = x.shape
    assert bsz == 1 and seq % S5_ROWS == 0 and seq % ATT_ROWS == 0 and seq % FFN_ROWS == 0
    blocks = seq // S5_ROWS
    h = x.reshape(blocks, V7X_SUBLANES, S5_SEG, D_MODEL).transpose(0, 2, 1, 3).reshape(seq, D_MODEL)
    for layer in range(N_A_LAYERS):
        params = _s5_params(s5_log_dt[layer], s5_a_re[layer], s5_a_im[layer], s5_b_re[layer],
                            s5_b_im[layer], s5_c_re[layer], s5_c_im[layer])
        h = _s5_layer(h, norm_mix_pre[layer], norm_mix_post[layer], params, s5_d[layer],
                      s5_w_glu[layer], s5_b_glu[layer])
        h = _ffn_layer(h, norm_ffn_pre[layer], norm_ffn_post[layer], w_ffn_in[layer], w_ffn_out[layer])
    h = h.reshape(blocks, S5_SEG, V7X_SUBLANES, D_MODEL).transpose(0, 2, 1, 3).reshape(seq, D_MODEL)

    kt = v = None
    for layer in range(N_A_LAYERS, DEPTH):
        b = layer - N_A_LAYERS
        if b == 0:
            q, kt, v = _qkv_proj(h, norm_mix_pre[layer], w_q[b], kv_norm, w_k, w_v)
        else:
            q = _q_proj(h, norm_mix_pre[layer], w_q[b])
        att = _attention(q, kt, v)
        h = _ffn_layer(h, norm_ffn_pre[layer], norm_ffn_post[layer], w_ffn_in[layer], w_ffn_out[layer],
                       att=att, w_o=w_o[b], g_mix=norm_mix_post[layer])
    return h.reshape(bsz, seq, D_MODEL)
```

```python
import functools
import math

import jax
import jax.numpy as jnp
from jax import lax
from jax.experimental import pallas as pl
from jax.experimental.pallas import tpu as pltpu

F32 = jnp.float32
BF16 = jnp.bfloat16

D_MODEL = 1024
DEPTH = 4
N_A_LAYERS = DEPTH // 2
SSM_GROUP = 16
SSM_GROUPS = D_MODEL // SSM_GROUP
SSM_STATE = 64
N_HEADS = 16
HEAD_DIM = D_MODEL // N_HEADS
D_FF = 2816
EPS = 1e-6

V7X_LANES = 128
V7X_SUBLANES = 8
V7X_MXU_DIM = 256
V7X_VMEM_BYTES = 64 * 1024 * 1024

S5_KTILES = D_MODEL // V7X_MXU_DIM
S5_TILE_GROUPS = V7X_MXU_DIM // SSM_GROUP
S5_TILE_STATES = S5_TILE_GROUPS * SSM_STATE
S5_SEG = 64
S5_ROWS = V7X_SUBLANES * S5_SEG
S5_COLS = 512

FFN_ROWS = 512
PROJ_ROWS = 512

ATT_KEYS = 128
ATT_QBLOCKS = 4
ATT_ROWS = ATT_QBLOCKS * ATT_KEYS
HEADS_PER_STEP = V7X_LANES // HEAD_DIM
ATT_PAIR_ROWS = HEADS_PER_STEP * ATT_KEYS
LOG2_E = math.log2(math.e)
ATT_LOG2_CUTOFF = -104.0 * LOG2_E


def _vmem_limit(nbytes):
    return int(min(nbytes, V7X_VMEM_BYTES - 6 * 1024 * 1024))


def _rms(x, g):
    return x * lax.rsqrt(jnp.mean(x * x, axis=-1, keepdims=True) + EPS) * g


def _dot(a, b):
    return jnp.dot(a, b, preferred_element_type=F32)


def _neg_abs(x):
    return -jnp.abs(x)


def _const_spec(shape):
    zeros = (0,) * len(shape)
    return pl.BlockSpec(shape, lambda *_: zeros, pipeline_mode=pl.Buffered(1))


def _s5_kernel(h_ref, gpre_ref, gpost_ref, wb_ref, wc_ref, lre_ref, lim_ref, mre_ref, mim_ref,
               pwre_ref, pwim_ref, d_ref, wglu_ref, bglu_ref, o_ref,
               xre_ref, xim_ref, cre_ref, cim_ref, y_ref):
    @pl.when(pl.program_id(0) == 0)
    def _():
        cre_ref[...] = jnp.zeros_like(cre_ref)
        cim_ref[...] = jnp.zeros_like(cim_ref)

    h = h_ref[...]
    u = _rms(h, gpre_ref[...])
    u_bf = u.astype(BF16)
    zeros = jnp.zeros((V7X_SUBLANES, S5_COLS), F32)
    row = lax.broadcasted_iota(jnp.int32, (V7X_SUBLANES, S5_COLS), 0)

    for kt in range(S5_KTILES):
        bu = _dot(u_bf[:, kt * V7X_MXU_DIM:(kt + 1) * V7X_MXU_DIM], wb_ref[kt])
        xre_ref[...] = bu[:, :S5_TILE_STATES]
        xim_ref[...] = bu[:, S5_TILE_STATES:]

        for cg in range(S5_TILE_STATES // S5_COLS):
            cs = slice(cg * S5_COLS, (cg + 1) * S5_COLS)
            lre = jnp.broadcast_to(lre_ref[kt, :, cs], (V7X_SUBLANES, S5_COLS))
            lim = jnp.broadcast_to(lim_ref[kt, :, cs], (V7X_SUBLANES, S5_COLS))

            def local_scan(i, carry):
                xr, xi = carry
                r0 = pl.multiple_of(i * V7X_SUBLANES, V7X_SUBLANES)
                nr = lre * xr - lim * xi + xre_ref[pl.ds(r0, V7X_SUBLANES), cs]
                ni = lre * xi + lim * xr + xim_ref[pl.ds(r0, V7X_SUBLANES), cs]
                xre_ref[pl.ds(r0, V7X_SUBLANES), cs] = nr
                xim_ref[pl.ds(r0, V7X_SUBLANES), cs] = ni
                return nr, ni

            er, ei = lax.fori_loop(0, S5_SEG, local_scan, (zeros, zeros), unroll=4)

            mre = mre_ref[kt, :, cs]
            mim = mim_ref[kt, :, cs]
            cur_r = cre_ref[kt, :, cs]
            cur_i = cim_ref[kt, :, cs]
            xin_r = zeros
            xin_i = zeros
            for r in range(V7X_SUBLANES):
                xin_r = jnp.where(row == r, cur_r, xin_r)
                xin_i = jnp.where(row == r, cur_i, xin_i)
                nxt_r = mre * cur_r - mim * cur_i + er[r:r + 1, :]
                nxt_i = mre * cur_i + mim * cur_r + ei[r:r + 1, :]
                cur_r, cur_i = nxt_r, nxt_i
            cre_ref[kt, :, cs] = cur_r
            cim_ref[kt, :, cs] = cur_i

            def add_incoming(i, c):
                r0 = pl.multiple_of(i * V7X_SUBLANES, V7X_SUBLANES)
                pr = pwre_ref[kt, pl.ds(i, 1), cs]
                pi = pwim_ref[kt, pl.ds(i, 1), cs]
                xre_ref[pl.ds(r0, V7X_SUBLANES), cs] += pr * xin_r - pi * xin_i
                xim_ref[pl.ds(r0, V7X_SUBLANES), cs] += pr * xin_i + pi * xin_r
                return c

            lax.fori_loop(0, S5_SEG, add_incoming, 0, unroll=4)

        y_ref[:, kt * V7X_MXU_DIM:(kt + 1) * V7X_MXU_DIM] = (
            _dot(xre_ref[...].astype(BF16), wc_ref[kt, :S5_TILE_STATES, :])
            + _dot(xim_ref[...].astype(BF16), wc_ref[kt, S5_TILE_STATES:, :]))

    y = y_ref[...] + d_ref[...] * u
    a = jax.nn.gelu(y).astype(BF16)
    z = _dot(a, wglu_ref[...]) + bglu_ref[...]
    mix = z[:, :D_MODEL] * jax.nn.sigmoid(z[:, D_MODEL:])
    o_ref[...] = h + _rms(mix, gpost_ref[...])


def _s5_params(log_dt, a_re, a_im, b_re, b_im, c_re, c_im):
    dt = jnp.exp(log_dt.astype(F32))[:, None]
    a_re = a_re.astype(F32)
    a_im = a_im.astype(F32)
    mag = jnp.exp(a_re * dt)
    lam_re = mag * jnp.cos(a_im * dt)
    lam_im = mag * jnp.sin(a_im * dt)
    den = a_re * a_re + a_im * a_im
    nr = lam_re - 1.0
    coef_re = (nr * a_re + lam_im * a_im) / den
    coef_im = (lam_im * a_re - nr * a_im) / den
    bb_re = coef_re[..., None] * b_re - coef_im[..., None] * b_im
    bb_im = coef_re[..., None] * b_im + coef_im[..., None] * b_re

    eye = jnp.eye(S5_TILE_GROUPS, dtype=F32)

    def in_proj(b):
        b = b.reshape(S5_KTILES, S5_TILE_GROUPS, SSM_STATE, SSM_GROUP)
        w = jnp.einsum('kgpc,gh->kgchp', b, eye)
        return w.reshape(S5_KTILES, V7X_MXU_DIM, S5_TILE_STATES)

    def out_proj(c):
        c = c.astype(F32).reshape(S5_KTILES, S5_TILE_GROUPS, SSM_GROUP, SSM_STATE)
        w = jnp.einsum('kgcp,gh->kgphc', c, eye)
        return w.reshape(S5_KTILES, S5_TILE_STATES, V7X_MXU_DIM)

    wb = jnp.concatenate([in_proj(bb_re), in_proj(bb_im)], axis=2).astype(BF16)
    wc = jnp.concatenate([out_proj(c_re), -out_proj(c_im)], axis=1).astype(BF16)

    def tile(v):
        return v.reshape(S5_KTILES, 1, S5_TILE_STATES)

    def power(n):
        m = jnp.exp(n * a_re * dt)
        return m * jnp.cos(n * a_im * dt), m * jnp.sin(n * a_im * dt)

    m_re, m_im = power(float(S5_SEG))
    steps = jnp.arange(1, S5_SEG + 1, dtype=F32)[:, None, None]
    pm = jnp.exp(steps * (a_re * dt))
    pw_re = pm * jnp.cos(steps * (a_im * dt))
    pw_im = pm * jnp.sin(steps * (a_im * dt))

    def tile_steps(v):
        return v.reshape(S5_SEG, S5_KTILES, S5_TILE_STATES).transpose(1, 0, 2)

    return (wb, wc, tile(lam_re), tile(lam_im), tile(m_re), tile(m_im),
            tile_steps(pw_re), tile_steps(pw_im))


def _s5_layer(h, g_pre, g_post, params, d_skip, w_glu, b_glu):
    seq = h.shape[0]
    wb, wc, lre, lim, mre, mim, pwre, pwim = params
    row_spec = pl.BlockSpec((S5_ROWS, D_MODEL), lambda i: (i, 0))
    args = (h, g_pre.reshape(1, D_MODEL), g_post.reshape(1, D_MODEL), wb, wc, lre, lim, mre, mim,
            pwre, pwim, d_skip.reshape(1, D_MODEL).astype(F32), w_glu.astype(BF16),
            b_glu.reshape(1, 2 * D_MODEL).astype(F32))
    in_specs = [row_spec] + [_const_spec(a.shape) for a in args[1:]]
    return pl.pallas_call(
        _s5_kernel,
        name="s5_mixer",
        grid=(seq // S5_ROWS,),
        in_specs=in_specs,
        out_specs=row_spec,
        out_shape=jax.ShapeDtypeStruct((seq, D_MODEL), F32),
        scratch_shapes=[
            pltpu.VMEM((S5_ROWS, S5_TILE_STATES), F32),
            pltpu.VMEM((S5_ROWS, S5_TILE_STATES), F32),
            pltpu.VMEM((S5_KTILES, 1, S5_TILE_STATES), F32),
            pltpu.VMEM((S5_KTILES, 1, S5_TILE_STATES), F32),
            pltpu.VMEM((S5_ROWS, D_MODEL), F32),
        ],
        compiler_params=pltpu.CompilerParams(
            dimension_semantics=("arbitrary",), vmem_limit_bytes=_vmem_limit(56 * 1024 * 1024)),
    )(*args)


def _ffn_body(h, gpre_ref, gpost_ref, win_ref, wout_ref, o_ref):
    xn = _rms(h, gpre_ref[...]).astype(BF16)
    hd = _dot(xn, win_ref[...])
    a = (jax.nn.silu(hd[:, :D_FF]) * hd[:, D_FF:]).astype(BF16)
    f = _dot(a, wout_ref[...])
    o_ref[...] = h + _rms(f, gpost_ref[...])


def _ffn_kernel(h_ref, gpre_ref, gpost_ref, win_ref, wout_ref, o_ref):
    _ffn_body(h_ref[...], gpre_ref, gpost_ref, win_ref, wout_ref, o_ref)


def _oproj_ffn_kernel(h_ref, att_ref, wo_ref, gmix_ref, gpre_ref, gpost_ref, win_ref, wout_ref, o_ref):
    h = h_ref[...] + _rms(_dot(att_ref[...], wo_ref[...]), gmix_ref[...])
    _ffn_body(h, gpre_ref, gpost_ref, win_ref, wout_ref, o_ref)


def _ffn_layer(h, g_pre, g_post, w_in, w_out, att=None, w_o=None, g_mix=None):
    seq = h.shape[0]
    row_spec = pl.BlockSpec((FFN_ROWS, D_MODEL), lambda i: (i, 0))
    tail = (g_pre.reshape(1, D_MODEL), g_post.reshape(1, D_MODEL), w_in.astype(BF16), w_out.astype(BF16))
    if att is None:
        kern, args, specs = _ffn_kernel, (h,) + tail, [row_spec]
    else:
        head = (att, w_o.astype(BF16), g_mix.reshape(1, D_MODEL))
        kern, args = _oproj_ffn_kernel, (h,) + head + tail
        specs = [row_spec, row_spec, _const_spec(head[1].shape), _const_spec(head[2].shape)]
    specs = specs + [_const_spec(a.shape) for a in tail]
    return pl.pallas_call(
        kern,
        name="ffn",
        grid=(seq // FFN_ROWS,),
        in_specs=specs,
        out_specs=row_spec,
        out_shape=jax.ShapeDtypeStruct((seq, D_MODEL), F32),
        compiler_params=pltpu.CompilerParams(
            dimension_semantics=("arbitrary",), vmem_limit_bytes=_vmem_limit(56 * 1024 * 1024)),
    )(*args)


def _q_of(h, gq_ref, wq_ref):
    u = _rms(h, gq_ref[...]).astype(BF16)
    return (_dot(u, wq_ref[...]) * (HEAD_DIM ** -0.5 * LOG2_E)).astype(BF16)


def _qkv_kernel(h_ref, gq_ref, wq_ref, gkv_ref, wkt_ref, wv_ref, q_ref, kt_ref, v_ref):
    h = h_ref[...]
    q_ref[...] = _q_of(h, gq_ref, wq_ref)
    hk = _rms(h, gkv_ref[...]).astype(BF16)
    kt_ref[...] = lax.dot_general(wkt_ref[...], hk, (((1,), (1,)), ((), ())),
                                  preferred_element_type=F32).astype(BF16)
    v_ref[...] = _dot(hk, wv_ref[...]).astype(BF16)


def _q_kernel(h_ref, gq_ref, wq_ref, q_ref):
    q_ref[...] = _q_of(h_ref[...], gq_ref, wq_ref)


def _qkv_proj(h, g_q, w_q, g_kv, w_k, w_v):
    seq = h.shape[0]
    row_spec = pl.BlockSpec((PROJ_ROWS, D_MODEL), lambda i: (i, 0))
    col_spec = pl.BlockSpec((D_MODEL, PROJ_ROWS), lambda i: (0, i))
    args = (h, g_q.reshape(1, D_MODEL), w_q.astype(BF16), g_kv.reshape(1, D_MODEL),
            w_k.T.astype(BF16), w_v.astype(BF16))
    return pl.pallas_call(
        _qkv_kernel,
        name="qkv_proj",
        grid=(seq // PROJ_ROWS,),
        in_specs=[row_spec] + [_const_spec(a.shape) for a in args[1:]],
        out_specs=[row_spec, col_spec, row_spec],
        out_shape=[jax.ShapeDtypeStruct((seq, D_MODEL), BF16),
                   jax.ShapeDtypeStruct((D_MODEL, seq), BF16),
                   jax.ShapeDtypeStruct((seq, D_MODEL), BF16)],
        compiler_params=pltpu.CompilerParams(
            dimension_semantics=("arbitrary",), vmem_limit_bytes=_vmem_limit(40 * 1024 * 1024)),
    )(*args)


def _q_proj(h, g_q, w_q):
    seq = h.shape[0]
    row_spec = pl.BlockSpec((PROJ_ROWS, D_MODEL), lambda i: (i, 0))
    args = (h, g_q.reshape(1, D_MODEL), w_q.astype(BF16))
    return pl.pallas_call(
        _q_kernel,
        name="q_proj",
        grid=(seq // PROJ_ROWS,),
        in_specs=[row_spec] + [_const_spec(a.shape) for a in args[1:]],
        out_specs=row_spec,
        out_shape=jax.ShapeDtypeStruct((seq, D_MODEL), BF16),
        compiler_params=pltpu.CompilerParams(
            dimension_semantics=("arbitrary",), vmem_limit_bytes=_vmem_limit(40 * 1024 * 1024)),
    )(*args)


def _attn_kernel(q_ref, kt_ref, v_ref, o_ref, qm_ref, acc_ref, lr_ref):
    tile = pl.program_id(1)
    lane = lax.broadcasted_iota(jnp.int32, (ATT_KEYS, V7X_LANES), 1)
    head0 = lane < HEAD_DIM
    srow = lax.broadcasted_iota(jnp.int32, (ATT_PAIR_ROWS, ATT_KEYS), 0)
    skey = lax.broadcasted_iota(jnp.int32, (ATT_PAIR_ROWS, ATT_KEYS), 1)
    causal = jnp.where(srow >= ATT_KEYS, srow - ATT_KEYS, srow) > skey
    trow = lax.broadcasted_iota(jnp.int32, (2 * ATT_KEYS, 2 * ATT_KEYS), 0)
    tcol = lax.broadcasted_iota(jnp.int32, (2 * ATT_KEYS, 2 * ATT_KEYS), 1)
    tkey = jnp.where(trow >= ATT_KEYS, trow - ATT_KEYS, trow)
    tri_ones = jnp.where((tkey >= tcol) | (tcol >= ATT_KEYS), 1.0, 0.0).astype(BF16)

    for qb in range(ATT_QBLOCKS):
        q = q_ref[qb * ATT_KEYS:(qb + 1) * ATT_KEYS, :]
        zero = jnp.zeros_like(q)
        qm_ref[qb] = jnp.concatenate([jnp.where(head0, q, zero), jnp.where(head0, zero, q)], axis=0)

    def step(kv_blocks, keep, state):
        starts = [pl.multiple_of(kb * ATT_KEYS, ATT_KEYS) for kb in kv_blocks]
        zs = [_dot(qm_ref[qb], kt_ref[:, pl.ds(starts[qb], ATT_KEYS)]) for qb in range(ATT_QBLOCKS)]
        sps = []
        for qb in range(ATT_QBLOCKS):
            sp = jnp.maximum(zs[qb], 0.0) + jnp.log2(1.0 + jnp.exp2(_neg_abs(zs[qb])))
            sps.append(sp if keep is None else jnp.where(keep[qb], sp, 0.0))
        sp = jnp.concatenate(sps, axis=0)
        sp_hi = sp.astype(BF16)
        sp_lo = (sp - sp_hi.astype(F32)).astype(BF16)
        sums = _dot(jnp.concatenate([sp_hi, sp_lo], axis=1), tri_ones)
        out = []
        for qb in range(ATT_QBLOCKS):
            rows = slice(qb * ATT_PAIR_ROWS, (qb + 1) * ATT_PAIR_ROWS)
            logw = zs[qb] - sums[rows, :ATT_KEYS]
            total = sums[rows, ATT_KEYS:]
            if state is not None:
                acc, log_rest = state[qb]
                logw = logw + log_rest
            w = jnp.exp2(logw)
            if keep is not None:
                w = jnp.where(keep[qb], w, 0.0)
            pv = _dot(w.astype(BF16), v_ref[pl.ds(starts[qb], ATT_KEYS), :])
            out.append((pv, -total) if state is None else (acc + pv, log_rest - total))
        return out

    def load_state():
        return [(acc_ref[qb], lr_ref[qb]) for qb in range(ATT_QBLOCKS)]

    def store_state(state):
        top = None
        for qb, (acc, log_rest) in enumerate(state):
            acc_ref[qb] = acc
            lr_ref[qb] = log_rest
            top = log_rest if top is None else jnp.maximum(top, log_rest)
        return jnp.max(top)

    first_block = tile * ATT_QBLOCKS
    top = store_state(step([first_block + qb for qb in range(ATT_QBLOCKS)], [causal] * ATT_QBLOCKS, None))

    def walk_full(carry):
        s, _ = carry
        return s + 1, store_state(step([first_block + qb - s for qb in range(ATT_QBLOCKS)], None, load_state()))

    def walk_ragged(carry):
        s, _ = carry
        blocks = [first_block + qb - s for qb in range(ATT_QBLOCKS)]
        state = step([jnp.maximum(kb, 0) for kb in blocks], [kb >= 0 for kb in blocks], load_state())
        return s + 1, store_state(state)

    carry = lax.while_loop(lambda c: (c[0] <= first_block) & (c[1] > ATT_LOG2_CUTOFF), walk_full,
                           (jnp.int32(1), top))
    lax.while_loop(lambda c: (c[0] < first_block + ATT_QBLOCKS) & (c[1] > ATT_LOG2_CUTOFF), walk_ragged, carry)

    for qb in range(ATT_QBLOCKS):
        acc = acc_ref[qb]
        out = jnp.where(head0, acc[:ATT_KEYS], acc[ATT_KEYS:])
        o_ref[qb * ATT_KEYS:(qb + 1) * ATT_KEYS, :] = out.astype(BF16)


def _attention(q, kt, v):
    seq = q.shape[0]
    assert HEADS_PER_STEP == 2
    state = pltpu.VMEM((ATT_QBLOCKS, ATT_PAIR_ROWS, ATT_KEYS), F32)
    stacked_q = pltpu.VMEM((ATT_QBLOCKS, ATT_PAIR_ROWS, V7X_LANES), BF16)
    return pl.pallas_call(
        _attn_kernel,
        name="stickbreaking_attention",
        grid=(N_HEADS // HEADS_PER_STEP, seq // ATT_ROWS),
        in_specs=[pl.BlockSpec((ATT_ROWS, V7X_LANES), lambda p, i: (i, p)),
                  pl.BlockSpec((V7X_LANES, seq), lambda p, i: (p, 0)),
                  pl.BlockSpec((seq, V7X_LANES), lambda p, i: (0, p))],
        out_specs=pl.BlockSpec((ATT_ROWS, V7X_LANES), lambda p, i: (i, p)),
        out_shape=jax.ShapeDtypeStruct((seq, D_MODEL), BF16),
        scratch_shapes=[stacked_q, state, state],
        compiler_params=pltpu.CompilerParams(
            dimension_semantics=("arbitrary", "arbitrary"), vmem_limit_bytes=_vmem_limit(40 * 1024 * 1024)),
    )(q, kt, v)


def kernel(x, norm_mix_pre, norm_mix_post, norm_ffn_pre, norm_ffn_post, w_ffn_in, w_ffn_out, s5_log_dt, s5_a_re, s5_a_im, s5_b_re, s5_b_im, s5_c_re, s5_c_im, s5_d, s5_w_glu, s5_b_glu, kv_norm, w_k, w_v, w_q, w_o):
    bsz, seq, _ = x.shape
    assert bsz == 1 and seq % S5_ROWS == 0 and seq % ATT_ROWS == 0 and seq % FFN_ROWS == 0
    blocks = seq // S5_ROWS
    h = x.reshape(blocks, V7X_SUBLANES, S5_SEG, D_MODEL).transpose(0, 2, 1, 3).reshape(seq, D_MODEL)
    for layer in range(N_A_LAYERS):
        params = _s5_params(s5_log_dt[layer], s5_a_re[layer], s5_a_im[layer], s5_b_re[layer],
                            s5_b_im[layer], s5_c_re[layer], s5_c_im[layer])
        h = _s5_layer(h, norm_mix_pre[layer], norm_mix_post[layer], params, s5_d[layer],
                      s5_w_glu[layer], s5_b_glu[layer])
        h = _ffn_layer(h, norm_ffn_pre[layer], norm_ffn_post[layer], w_ffn_in[layer], w_ffn_out[layer])
    h = h.reshape(blocks, S5_SEG, V7X_SUBLANES, D_MODEL).transpose(0, 2, 1, 3).reshape(seq, D_MODEL)

    kt = v = None
    for layer in range(N_A_LAYERS, DEPTH):
        b = layer - N_A_LAYERS
        if b == 0:
            q, kt, v = _qkv_proj(h, norm_mix_pre[layer], w_q[b], kv_norm, w_k, w_v)
        else:
            q = _q_proj(h, norm_mix_pre[layer], w_q[b])
        att = _attention(q, kt, v)
        h = _ffn_layer(h, norm_ffn_pre[layer], norm_ffn_post[layer], w_ffn_in[layer], w_ffn_out[layer],
                       att=att, w_o=w_o[b], g_mix=norm_mix_post[layer])
    return h.reshape(bsz, seq, D_MODEL)
```

```python
import functools
import math

import jax
import jax.numpy as jnp
from jax import lax
from jax.experimental import pallas as pl
from jax.experimental.pallas import tpu as pltpu

F32 = jnp.float32
BF16 = jnp.bfloat16

D_MODEL = 1024
DEPTH = 4
N_A_LAYERS = DEPTH // 2
SSM_GROUP = 16
SSM_GROUPS = D_MODEL // SSM_GROUP
SSM_STATE = 64
N_HEADS = 16
HEAD_DIM = D_MODEL // N_HEADS
D_FF = 2816
EPS = 1e-6

V7X_LANES = 128
V7X_SUBLANES = 8
V7X_MXU_DIM = 256
V7X_VMEM_BYTES = 64 * 1024 * 1024

S5_KTILES = D_MODEL // V7X_MXU_DIM
S5_TILE_GROUPS = V7X_MXU_DIM // SSM_GROUP
S5_TILE_STATES = S5_TILE_GROUPS * SSM_STATE
S5_SEG = 64
S5_ROWS = V7X_SUBLANES * S5_SEG
S5_COLS = 512

FFN_ROWS = 512
PROJ_ROWS = 512

ATT_KEYS = 128
ATT_QBLOCKS = 8
ATT_ROWS = ATT_QBLOCKS * ATT_KEYS
HEADS_PER_STEP = V7X_LANES // HEAD_DIM
ATT_PAIR_ROWS = HEADS_PER_STEP * ATT_KEYS
LOG2_E = math.log2(math.e)
ATT_LOG2_CUTOFF = -104.0 * LOG2_E


def _vmem_limit(nbytes):
    return int(min(nbytes, V7X_VMEM_BYTES - 6 * 1024 * 1024))


def _rms(x, g):
    return x * lax.rsqrt(jnp.mean(x * x, axis=-1, keepdims=True) + EPS) * g


def _dot(a, b):
    return jnp.dot(a, b, preferred_element_type=F32)


def _neg_abs(x):
    return -jnp.abs(x)


def _const_spec(shape):
    zeros = (0,) * len(shape)
    return pl.BlockSpec(shape, lambda *_: zeros, pipeline_mode=pl.Buffered(1))


def _s5_kernel(h_ref, gpre_ref, gpost_ref, wb_ref, wc_ref, lre_ref, lim_ref, mre_ref, mim_ref,
               pwre_ref, pwim_ref, d_ref, wglu_ref, bglu_ref, o_ref,
               xre_ref, xim_ref, cre_ref, cim_ref, y_ref):
    @pl.when(pl.program_id(0) == 0)
    def _():
        cre_ref[...] = jnp.zeros_like(cre_ref)
        cim_ref[...] = jnp.zeros_like(cim_ref)

    h = h_ref[...]
    u = _rms(h, gpre_ref[...])
    u_bf = u.astype(BF16)
    zeros = jnp.zeros((V7X_SUBLANES, S5_COLS), F32)
    row = lax.broadcasted_iota(jnp.int32, (V7X_SUBLANES, S5_COLS), 0)

    for kt in range(S5_KTILES):
        bu = _dot(u_bf[:, kt * V7X_MXU_DIM:(kt + 1) * V7X_MXU_DIM], wb_ref[kt])
        xre_ref[...] = bu[:, :S5_TILE_STATES]
        xim_ref[...] = bu[:, S5_TILE_STATES:]

        for cg in range(S5_TILE_STATES // S5_COLS):
            cs = slice(cg * S5_COLS, (cg + 1) * S5_COLS)
            lre = jnp.broadcast_to(lre_ref[kt, :, cs], (V7X_SUBLANES, S5_COLS))
            lim = jnp.broadcast_to(lim_ref[kt, :, cs], (V7X_SUBLANES, S5_COLS))

            def local_scan(i, carry):
                xr, xi = carry
                r0 = pl.multiple_of(i * V7X_SUBLANES, V7X_SUBLANES)
                nr = lre * xr - lim * xi + xre_ref[pl.ds(r0, V7X_SUBLANES), cs]
                ni = lre * xi + lim * xr + xim_ref[pl.ds(r0, V7X_SUBLANES), cs]
                xre_ref[pl.ds(r0, V7X_SUBLANES), cs] = nr
                xim_ref[pl.ds(r0, V7X_SUBLANES), cs] = ni
                return nr, ni

            er, ei = lax.fori_loop(0, S5_SEG, local_scan, (zeros, zeros), unroll=4)

            mre = mre_ref[kt, :, cs]
            mim = mim_ref[kt, :, cs]
            cur_r = cre_ref[kt, :, cs]
            cur_i = cim_ref[kt, :, cs]
            xin_r = zeros
            xin_i = zeros
            for r in range(V7X_SUBLANES):
                xin_r = jnp.where(row == r, cur_r, xin_r)
                xin_i = jnp.where(row == r, cur_i, xin_i)
                nxt_r = mre * cur_r - mim * cur_i + er[r:r + 1, :]
                nxt_i = mre * cur_i + mim * cur_r + ei[r:r + 1, :]
                cur_r, cur_i = nxt_r, nxt_i
            cre_ref[kt, :, cs] = cur_r
            cim_ref[kt, :, cs] = cur_i

            def add_incoming(i, c):
                r0 = pl.multiple_of(i * V7X_SUBLANES, V7X_SUBLANES)
                pr = pwre_ref[kt, pl.ds(i, 1), cs]
                pi = pwim_ref[kt, pl.ds(i, 1), cs]
                xre_ref[pl.ds(r0, V7X_SUBLANES), cs] += pr * xin_r - pi * xin_i
                xim_ref[pl.ds(r0, V7X_SUBLANES), cs] += pr * xin_i + pi * xin_r
                return c

            lax.fori_loop(0, S5_SEG, add_incoming, 0, unroll=4)

        y_ref[:, kt * V7X_MXU_DIM:(kt + 1) * V7X_MXU_DIM] = (
            _dot(xre_ref[...].astype(BF16), wc_ref[kt, :S5_TILE_STATES, :])
            + _dot(xim_ref[...].astype(BF16), wc_ref[kt, S5_TILE_STATES:, :]))

    y = y_ref[...] + d_ref[...] * u
    a = jax.nn.gelu(y).astype(BF16)
    z = _dot(a, wglu_ref[...]) + bglu_ref[...]
    mix = z[:, :D_MODEL] * jax.nn.sigmoid(z[:, D_MODEL:])
    o_ref[...] = h + _rms(mix, gpost_ref[...])


def _s5_params(log_dt, a_re, a_im, b_re, b_im, c_re, c_im):
    dt = jnp.exp(log_dt.astype(F32))[:, None]
    a_re = a_re.astype(F32)
    a_im = a_im.astype(F32)
    mag = jnp.exp(a_re * dt)
    lam_re = mag * jnp.cos(a_im * dt)
    lam_im = mag * jnp.sin(a_im * dt)
    den = a_re * a_re + a_im * a_im
    nr = lam_re - 1.0
    coef_re = (nr * a_re + lam_im * a_im) / den
    coef_im = (lam_im * a_re - nr * a_im) / den
    bb_re = coef_re[..., None] * b_re - coef_im[..., None] * b_im
    bb_im = coef_re[..., None] * b_im + coef_im[..., None] * b_re

    eye = jnp.eye(S5_TILE_GROUPS, dtype=F32)

    def in_proj(b):
        b = b.reshape(S5_KTILES, S5_TILE_GROUPS, SSM_STATE, SSM_GROUP)
        w = jnp.einsum('kgpc,gh->kgchp', b, eye)
        return w.reshape(S5_KTILES, V7X_MXU_DIM, S5_TILE_STATES)

    def out_proj(c):
        c = c.astype(F32).reshape(S5_KTILES, S5_TILE_GROUPS, SSM_GROUP, SSM_STATE)
        w = jnp.einsum('kgcp,gh->kgphc', c, eye)
        return w.reshape(S5_KTILES, S5_TILE_STATES, V7X_MXU_DIM)

    wb = jnp.concatenate([in_proj(bb_re), in_proj(bb_im)], axis=2).astype(BF16)
    wc = jnp.concatenate([out_proj(c_re), -out_proj(c_im)], axis=1).astype(BF16)

    def tile(v):
        return v.reshape(S5_KTILES, 1, S5_TILE_STATES)

    def power(n):
        m = jnp.exp(n * a_re * dt)
        return m * jnp.cos(n * a_im * dt), m * jnp.sin(n * a_im * dt)

    m_re, m_im = power(float(S5_SEG))
    steps = jnp.arange(1, S5_SEG + 1, dtype=F32)[:, None, None]
    pm = jnp.exp(steps * (a_re * dt))
    pw_re = pm * jnp.cos(steps * (a_im * dt))
    pw_im = pm * jnp.sin(steps * (a_im * dt))

    def tile_steps(v):
        return v.reshape(S5_SEG, S5_KTILES, S5_TILE_STATES).transpose(1, 0, 2)

    return (wb, wc, tile(lam_re), tile(lam_im), tile(m_re), tile(m_im),
            tile_steps(pw_re), tile_steps(pw_im))


def _s5_layer(h, g_pre, g_post, params, d_skip, w_glu, b_glu):
    seq = h.shape[0]
    wb, wc, lre, lim, mre, mim, pwre, pwim = params
    row_spec = pl.BlockSpec((S5_ROWS, D_MODEL), lambda i: (i, 0))
    args = (h, g_pre.reshape(1, D_MODEL), g_post.reshape(1, D_MODEL), wb, wc, lre, lim, mre, mim,
            pwre, pwim, d_skip.reshape(1, D_MODEL).astype(F32), w_glu.astype(BF16),
            b_glu.reshape(1, 2 * D_MODEL).astype(F32))
    in_specs = [row_spec] + [_const_spec(a.shape) for a in args[1:]]
    return pl.pallas_call(
        _s5_kernel,
        name="s5_mixer",
        grid=(seq // S5_ROWS,),
        in_specs=in_specs,
        out_specs=row_spec,
        out_shape=jax.ShapeDtypeStruct((seq, D_MODEL), F32),
        scratch_shapes=[
            pltpu.VMEM((S5_ROWS, S5_TILE_STATES), F32),
            pltpu.VMEM((S5_ROWS, S5_TILE_STATES), F32),
            pltpu.VMEM((S5_KTILES, 1, S5_TILE_STATES), F32),
            pltpu.VMEM((S5_KTILES, 1, S5_TILE_STATES), F32),
            pltpu.VMEM((S5_ROWS, D_MODEL), F32),
        ],
        compiler_params=pltpu.CompilerParams(
            dimension_semantics=("arbitrary",), vmem_limit_bytes=_vmem_limit(56 * 1024 * 1024)),
    )(*args)


def _ffn_body(h, gpre_ref, gpost_ref, win_ref, wout_ref, o_ref):
    xn = _rms(h, gpre_ref[...]).astype(BF16)
    hd = _dot(xn, win_ref[...])
    a = (jax.nn.silu(hd[:, :D_FF]) * hd[:, D_FF:]).astype(BF16)
    f = _dot(a, wout_ref[...])
    o_ref[...] = h + _rms(f, gpost_ref[...])


def _ffn_kernel(h_ref, gpre_ref, gpost_ref, win_ref, wout_ref, o_ref):
    _ffn_body(h_ref[...], gpre_ref, gpost_ref, win_ref, wout_ref, o_ref)


def _oproj_ffn_kernel(h_ref, att_ref, wo_ref, gmix_ref, gpre_ref, gpost_ref, win_ref, wout_ref, o_ref):
    h = h_ref[...] + _rms(_dot(att_ref[...], wo_ref[...]), gmix_ref[...])
    _ffn_body(h, gpre_ref, gpost_ref, win_ref, wout_ref, o_ref)


def _ffn_layer(h, g_pre, g_post, w_in, w_out, att=None, w_o=None, g_mix=None):
    seq = h.shape[0]
    row_spec = pl.BlockSpec((FFN_ROWS, D_MODEL), lambda i: (i, 0))
    tail = (g_pre.reshape(1, D_MODEL), g_post.reshape(1, D_MODEL), w_in.astype(BF16), w_out.astype(BF16))
    if att is None:
        kern, args, specs = _ffn_kernel, (h,) + tail, [row_spec]
    else:
        head = (att, w_o.astype(BF16), g_mix.reshape(1, D_MODEL))
        kern, args = _oproj_ffn_kernel, (h,) + head + tail
        specs = [row_spec, row_spec, _const_spec(head[1].shape), _const_spec(head[2].shape)]
    specs = specs + [_const_spec(a.shape) for a in tail]
    return pl.pallas_call(
        kern,
        name="ffn",
        grid=(seq // FFN_ROWS,),
        in_specs=specs,
        out_specs=row_spec,
        out_shape=jax.ShapeDtypeStruct((seq, D_MODEL), F32),
        compiler_params=pltpu.CompilerParams(
            dimension_semantics=("arbitrary",), vmem_limit_bytes=_vmem_limit(56 * 1024 * 1024)),
    )(*args)


def _q_of(h, gq_ref, wq_ref):
    u = _rms(h, gq_ref[...]).astype(BF16)
    return (_dot(u, wq_ref[...]) * (HEAD_DIM ** -0.5 * LOG2_E)).astype(BF16)


def _qkv_kernel(h_ref, gq_ref, wq_ref, gkv_ref, wkt_ref, wv_ref, q_ref, kt_ref, v_ref):
    h = h_ref[...]
    q_ref[...] = _q_of(h, gq_ref, wq_ref)
    hk = _rms(h, gkv_ref[...]).astype(BF16)
    kt_ref[...] = lax.dot_general(wkt_ref[...], hk, (((1,), (1,)), ((), ())),
                                  preferred_element_type=F32).astype(BF16)
    v_ref[...] = _dot(hk, wv_ref[...]).astype(BF16)


def _q_kernel(h_ref, gq_ref, wq_ref, q_ref):
    q_ref[...] = _q_of(h_ref[...], gq_ref, wq_ref)


def _qkv_proj(h, g_q, w_q, g_kv, w_k, w_v):
    seq = h.shape[0]
    row_spec = pl.BlockSpec((PROJ_ROWS, D_MODEL), lambda i: (i, 0))
    col_spec = pl.BlockSpec((D_MODEL, PROJ_ROWS), lambda i: (0, i))
    args = (h, g_q.reshape(1, D_MODEL), w_q.astype(BF16), g_kv.reshape(1, D_MODEL),
            w_k.T.astype(BF16), w_v.astype(BF16))
    return pl.pallas_call(
        _qkv_kernel,
        name="qkv_proj",
        grid=(seq // PROJ_ROWS,),
        in_specs=[row_spec] + [_const_spec(a.shape) for a in args[1:]],
        out_specs=[row_spec, col_spec, row_spec],
        out_shape=[jax.ShapeDtypeStruct((seq, D_MODEL), BF16),
                   jax.ShapeDtypeStruct((D_MODEL, seq), BF16),
                   jax.ShapeDtypeStruct((seq, D_MODEL), BF16)],
        compiler_params=pltpu.CompilerParams(
            dimension_semantics=("arbitrary",), vmem_limit_bytes=_vmem_limit(40 * 1024 * 1024)),
    )(*args)


def _q_proj(h, g_q, w_q):
    seq = h.shape[0]
    row_spec = pl.BlockSpec((PROJ_ROWS, D_MODEL), lambda i: (i, 0))
    args = (h, g_q.reshape(1, D_MODEL), w_q.astype(BF16))
    return pl.pallas_call(
        _q_kernel,
        name="q_proj",
        grid=(seq // PROJ_ROWS,),
        in_specs=[row_spec] + [_const_spec(a.shape) for a in args[1:]],
        out_specs=row_spec,
        out_shape=jax.ShapeDtypeStruct((seq, D_MODEL), BF16),
        compiler_params=pltpu.CompilerParams(
            dimension_semantics=("arbitrary",), vmem_limit_bytes=_vmem_limit(40 * 1024 * 1024)),
    )(*args)


def _attn_kernel(q_ref, kt_ref, v_ref, o_ref, qm_ref, acc_ref, lr_ref):
    tile = pl.program_id(1)
    lane = lax.broadcasted_iota(jnp.int32, (ATT_KEYS, V7X_LANES), 1)
    head0 = lane < HEAD_DIM
    srow = lax.broadcasted_iota(jnp.int32, (ATT_PAIR_ROWS, ATT_KEYS), 0)
    skey = lax.broadcasted_iota(jnp.int32, (ATT_PAIR_ROWS, ATT_KEYS), 1)
    causal = jnp.where(srow >= ATT_KEYS, srow - ATT_KEYS, srow) > skey
    trow = lax.broadcasted_iota(jnp.int32, (2 * ATT_KEYS, 2 * ATT_KEYS), 0)
    tcol = lax.broadcasted_iota(jnp.int32, (2 * ATT_KEYS, 2 * ATT_KEYS), 1)
    tkey = jnp.where(trow >= ATT_KEYS, trow - ATT_KEYS, trow)
    tri_ones = jnp.where((tkey >= tcol) | (tcol >= ATT_KEYS), 1.0, 0.0).astype(BF16)

    for qb in range(ATT_QBLOCKS):
        q = q_ref[qb * ATT_KEYS:(qb + 1) * ATT_KEYS, :]
        zero = jnp.zeros_like(q)
        qm_ref[qb] = jnp.concatenate([jnp.where(head0, q, zero), jnp.where(head0, zero, q)], axis=0)

    def step(kv_blocks, keep, state):
        starts = [pl.multiple_of(kb * ATT_KEYS, ATT_KEYS) for kb in kv_blocks]
        zs = [_dot(qm_ref[qb], kt_ref[:, pl.ds(starts[qb], ATT_KEYS)]) for qb in range(ATT_QBLOCKS)]
        sps = []
        for qb in range(ATT_QBLOCKS):
            sp = jnp.maximum(zs[qb], 0.0) + jnp.log2(1.0 + jnp.exp2(_neg_abs(zs[qb])))
            sps.append(sp if keep is None else jnp.where(keep[qb], sp, 0.0))
        sp = jnp.concatenate(sps, axis=0)
        sp_hi = sp.astype(BF16)
        sp_lo = (sp - sp_hi.astype(F32)).astype(BF16)
        sums = _dot(jnp.concatenate([sp_hi, sp_lo], axis=1), tri_ones)
        out = []
        for qb in range(ATT_QBLOCKS):
            rows = slice(qb * ATT_PAIR_ROWS, (qb + 1) * ATT_PAIR_ROWS)
            logw = zs[qb] - sums[rows, :ATT_KEYS]
            total = sums[rows, ATT_KEYS:]
            if state is not None:
                acc, log_rest = state[qb]
                logw = logw + log_rest
            w = jnp.exp2(logw)
            if keep is not None:
                w = jnp.where(keep[qb], w, 0.0)
            pv = _dot(w.astype(BF16), v_ref[pl.ds(starts[qb], ATT_KEYS), :])
            out.append((pv, -total) if state is None else (acc + pv, log_rest - total))
        return out

    def load_state():
        return [(acc_ref[qb], lr_ref[qb]) for qb in range(ATT_QBLOCKS)]

    def store_state(state):
        top = None
        for qb, (acc, log_rest) in enumerate(state):
            acc_ref[qb] = acc
            lr_ref[qb] = log_rest
            top = log_rest if top is None else jnp.maximum(top, log_rest)
        return jnp.max(top)

    first_block = tile * ATT_QBLOCKS
    top = store_state(step([first_block + qb for qb in range(ATT_QBLOCKS)], [causal] * ATT_QBLOCKS, None))

    def walk_full(carry):
        s, _ = carry
        return s + 1, store_state(step([first_block + qb - s for qb in range(ATT_QBLOCKS)], None, load_state()))

    def walk_ragged(carry):
        s, _ = carry
        blocks = [first_block + qb - s for qb in range(ATT_QBLOCKS)]
        state = step([jnp.maximum(kb, 0) for kb in blocks], [kb >= 0 for kb in blocks], load_state())
        return s + 1, store_state(state)

    carry = lax.while_loop(lambda c: (c[0] <= first_block) & (c[1] > ATT_LOG2_CUTOFF), walk_full,
                           (jnp.int32(1), top))
    lax.while_loop(lambda c: (c[0] < first_block + ATT_QBLOCKS) & (c[1] > ATT_LOG2_CUTOFF), walk_ragged, carry)

    for qb in range(ATT_QBLOCKS):
        acc = acc_ref[qb]
        out = jnp.where(head0, acc[:ATT_KEYS], acc[ATT_KEYS:])
        o_ref[qb * ATT_KEYS:(qb + 1) * ATT_KEYS, :] = out.astype(BF16)


def _attention(q, kt, v):
    seq = q.shape[0]
    assert HEADS_PER_STEP == 2
    state = pltpu.VMEM((ATT_QBLOCKS, ATT_PAIR_ROWS, ATT_KEYS), F32)
    stacked_q = pltpu.VMEM((ATT_QBLOCKS, ATT_PAIR_ROWS, V7X_LANES), BF16)
    return pl.pallas_call(
        _attn_kernel,
        name="stickbreaking_attention",
        grid=(N_HEADS // HEADS_PER_STEP, seq // ATT_ROWS),
        in_specs=[pl.BlockSpec((ATT_ROWS, V7X_LANES), lambda p, i: (i, p)),
                  pl.BlockSpec((V7X_LANES, seq), lambda p, i: (p, 0)),
                  pl.BlockSpec((seq, V7X_LANES), lambda p, i: (0, p))],
        out_specs=pl.BlockSpec((ATT_ROWS, V7X_LANES), lambda p, i: (i, p)),
        out_shape=jax.ShapeDtypeStruct((seq, D_MODEL), BF16),
        scratch_shapes=[stacked_q, state, state],
        compiler_params=pltpu.CompilerParams(
            dimension_semantics=("arbitrary", "arbitrary"), vmem_limit_bytes=_vmem_limit(40 * 1024 * 1024)),
    )(q, kt, v)


def kernel(x, norm_mix_pre, norm_mix_post, norm_ffn_pre, norm_ffn_post, w_ffn_in, w_ffn_out, s5_log_dt, s5_a_re, s5_a_im, s5_b_re, s5_b_im, s5_c_re, s5_c_im, s5_d, s5_w_glu, s5_b_glu, kv_norm, w_k, w_v, w_q, w_o):
    bsz, seq, _ = x.shape
    assert bsz == 1 and seq % S5_ROWS == 0 and seq % ATT_ROWS == 0 and seq % FFN_ROWS == 0
    blocks = seq // S5_ROWS
    h = x.reshape(blocks, V7X_SUBLANES, S5_SEG, D_MODEL).transpose(0, 2, 1, 3).reshape(seq, D_MODEL)
    for layer in range(N_A_LAYERS):
        params = _s5_params(s5_log_dt[layer], s5_a_re[layer], s5_a_im[layer], s5_b_re[layer],
                            s5_b_im[layer], s5_c_re[layer], s5_c_im[layer])
        h = _s5_layer(h, norm_mix_pre[layer], norm_mix_post[layer], params, s5_d[layer],
                      s5_w_glu[layer], s5_b_glu[layer])
        h = _ffn_layer(h, norm_ffn_pre[layer], norm_ffn_post[layer], w_ffn_in[layer], w_ffn_out[layer])
    h = h.reshape(blocks, S5_SEG, V7X_SUBLANES, D_MODEL).transpose(0, 2, 1, 3).reshape(seq, D_MODEL)

    kt = v = None
    for layer in range(N_A_LAYERS, DEPTH):
        b = layer - N_A_LAYERS
        if b == 0:
            q, kt, v = _qkv_proj(h, norm_mix_pre[layer], w_q[b], kv_norm, w_k, w_v)
        else:
            q = _q_proj(h, norm_mix_pre[layer], w_q[b])
        att = _attention(q, kt, v)
        h = _ffn_layer(h, norm_ffn_pre[layer], norm_ffn_post[layer], w_ffn_in[layer], w_ffn_out[layer],
                       att=att, w_o=w_o[b], g_mix=norm_mix_post[layer])
    return h.reshape(bsz, seq, D_MODEL)
```

```python
import functools
import math

import jax
import jax.numpy as jnp
from jax import lax
from jax.experimental import pallas as pl
from jax.experimental.pallas import tpu as pltpu

F32 = jnp.float32
BF16 = jnp.bfloat16

D_MODEL = 1024
DEPTH = 4
N_A_LAYERS = DEPTH // 2
SSM_GROUP = 16
SSM_GROUPS = D_MODEL // SSM_GROUP
SSM_STATE = 64
N_HEADS = 16
HEAD_DIM = D_MODEL // N_HEADS
D_FF = 2816
EPS = 1e-6

V7X_LANES = 128
V7X_SUBLANES = 8
BF16_ROWS = 2 * V7X_SUBLANES
V7X_MXU_DIM = 256
V7X_VMEM_BYTES = 64 * 1024 * 1024

S5_KTILES = D_MODEL // V7X_MXU_DIM
S5_TILE_GROUPS = V7X_MXU_DIM // SSM_GROUP
S5_TILE_STATES = S5_TILE_GROUPS * SSM_STATE
S5_SEG = 64
S5_ROWS = V7X_SUBLANES * S5_SEG
S5_COLS = 512

FFN_ROWS = 512
PROJ_ROWS = 512

ATT_KEYS = 128
ATT_QBLOCKS = 8
ATT_ROWS = ATT_QBLOCKS * ATT_KEYS
HEADS_PER_STEP = V7X_LANES // HEAD_DIM
ATT_PAIR_ROWS = HEADS_PER_STEP * ATT_KEYS
LOG2_E = math.log2(math.e)
ATT_LOG2_CUTOFF = -104.0 * LOG2_E


def _vmem_limit(nbytes):
    return int(min(nbytes, V7X_VMEM_BYTES - 6 * 1024 * 1024))


def _rms(x, g):
    return x * lax.rsqrt(jnp.mean(x * x, axis=-1, keepdims=True) + EPS) * g


def _dot(a, b):
    return jnp.dot(a, b, preferred_element_type=F32)


def _neg_abs(x):
    return -jnp.abs(x)


def _const_spec(shape):
    zeros = (0,) * len(shape)
    return pl.BlockSpec(shape, lambda *_: zeros, pipeline_mode=pl.Buffered(1))


def _s5_kernel(h_ref, gpre_ref, gpost_ref, wb_ref, wc_ref, lre_ref, lim_ref, mre_ref, mim_ref,
               d_ref, wglu_ref, bglu_ref, o_ref,
               xre_ref, xim_ref, xb_ref, cre_ref, cim_ref, y_ref):
    @pl.when(pl.program_id(0) == 0)
    def _():
        cre_ref[...] = jnp.zeros_like(cre_ref)
        cim_ref[...] = jnp.zeros_like(cim_ref)

    h = jnp.concatenate([h_ref[:, i * D_MODEL:(i + 1) * D_MODEL] for i in range(S5_SEG)], axis=0)
    u = _rms(h, gpre_ref[...])
    u_bf = u.astype(BF16)
    zeros = jnp.zeros((V7X_SUBLANES, S5_COLS), F32)
    row = lax.broadcasted_iota(jnp.int32, (V7X_SUBLANES, S5_COLS), 0)

    for kt in range(S5_KTILES):
        bu = _dot(u_bf[:, kt * V7X_MXU_DIM:(kt + 1) * V7X_MXU_DIM], wb_ref[kt])
        xre_ref[...] = bu[:, :S5_TILE_STATES]
        xim_ref[...] = bu[:, S5_TILE_STATES:]

        for cg in range(S5_TILE_STATES // S5_COLS):
            cs = slice(cg * S5_COLS, (cg + 1) * S5_COLS)
            lre = jnp.broadcast_to(lre_ref[kt, :, cs], (V7X_SUBLANES, S5_COLS))
            lim = jnp.broadcast_to(lim_ref[kt, :, cs], (V7X_SUBLANES, S5_COLS))

            def local_scan(i, carry):
                xr, xi = carry
                r0 = pl.multiple_of(i * V7X_SUBLANES, V7X_SUBLANES)
                nr = lre * xr - lim * xi + xre_ref[pl.ds(r0, V7X_SUBLANES), cs]
                ni = lre * xi + lim * xr + xim_ref[pl.ds(r0, V7X_SUBLANES), cs]
                xre_ref[pl.ds(r0, V7X_SUBLANES), cs] = nr
                xim_ref[pl.ds(r0, V7X_SUBLANES), cs] = ni
                return nr, ni

            er, ei = lax.fori_loop(0, S5_SEG, local_scan, (zeros, zeros), unroll=4)

            mre = mre_ref[kt, :, cs]
            mim = mim_ref[kt, :, cs]
            cur_r = cre_ref[kt, :, cs]
            cur_i = cim_ref[kt, :, cs]
            xin_r = zeros
            xin_i = zeros
            for r in range(V7X_SUBLANES):
                xin_r = jnp.where(row == r, cur_r, xin_r)
                xin_i = jnp.where(row == r, cur_i, xin_i)
                nxt_r = mre * cur_r - mim * cur_i + er[r:r + 1, :]
                nxt_i = mre * cur_i + mim * cur_r + ei[r:r + 1, :]
                cur_r, cur_i = nxt_r, nxt_i
            cre_ref[kt, :, cs] = cur_r
            cim_ref[kt, :, cs] = cur_i

            cs_im = slice(S5_TILE_STATES + cg * S5_COLS, S5_TILE_STATES + (cg + 1) * S5_COLS)

            def add_incoming(j, carry):
                cr, ci = carry
                r0 = pl.multiple_of(j * BF16_ROWS, BF16_ROWS)
                full_r, full_i = [], []
                for half in range(BF16_ROWS // V7X_SUBLANES):
                    cr, ci = lre * cr - lim * ci, lre * ci + lim * cr
                    rows = pl.ds(r0 + half * V7X_SUBLANES, V7X_SUBLANES)
                    full_r.append(xre_ref[rows, cs] + cr)
                    full_i.append(xim_ref[rows, cs] + ci)
                xb_ref[pl.ds(r0, BF16_ROWS), cs] = jnp.concatenate(full_r, axis=0).astype(BF16)
                xb_ref[pl.ds(r0, BF16_ROWS), cs_im] = jnp.concatenate(full_i, axis=0).astype(BF16)
                return cr, ci

            lax.fori_loop(0, S5_ROWS // BF16_ROWS, add_incoming, (xin_r, xin_i), unroll=2)

        y_ref[:, kt * V7X_MXU_DIM:(kt + 1) * V7X_MXU_DIM] = _dot(xb_ref[...], wc_ref[kt])

    y = y_ref[...] + d_ref[...] * u
    a = jax.nn.gelu(y).astype(BF16)
    z = _dot(a, wglu_ref[...]) + bglu_ref[...]
    mix = z[:, :D_MODEL] * jax.nn.sigmoid(z[:, D_MODEL:])
    out = h + _rms(mix, gpost_ref[...])
    for i in range(S5_SEG):
        o_ref[:, i * D_MODEL:(i + 1) * D_MODEL] = out[i * V7X_SUBLANES:(i + 1) * V7X_SUBLANES, :]


def _s5_params(log_dt, a_re, a_im, b_re, b_im, c_re, c_im):
    dt = jnp.exp(log_dt.astype(F32))[:, None]
    a_re = a_re.astype(F32)
    a_im = a_im.astype(F32)
    mag = jnp.exp(a_re * dt)
    lam_re = mag * jnp.cos(a_im * dt)
    lam_im = mag * jnp.sin(a_im * dt)
    den = a_re * a_re + a_im * a_im
    nr = lam_re - 1.0
    coef_re = (nr * a_re + lam_im * a_im) / den
    coef_im = (lam_im * a_re - nr * a_im) / den
    bb_re = coef_re[..., None] * b_re - coef_im[..., None] * b_im
    bb_im = coef_re[..., None] * b_im + coef_im[..., None] * b_re

    eye = jnp.eye(S5_TILE_GROUPS, dtype=F32)

    def in_proj(b):
        b = b.reshape(S5_KTILES, S5_TILE_GROUPS, SSM_STATE, SSM_GROUP)
        w = jnp.einsum('kgpc,gh->kgchp', b, eye)
        return w.reshape(S5_KTILES, V7X_MXU_DIM, S5_TILE_STATES)

    def out_proj(c):
        c = c.astype(F32).reshape(S5_KTILES, S5_TILE_GROUPS, SSM_GROUP, SSM_STATE)
        w = jnp.einsum('kgcp,gh->kgphc', c, eye)
        return w.reshape(S5_KTILES, S5_TILE_STATES, V7X_MXU_DIM)

    wb = jnp.concatenate([in_proj(bb_re), in_proj(bb_im)], axis=2).astype(BF16)
    wc = jnp.concatenate([out_proj(c_re), -out_proj(c_im)], axis=1).astype(BF16)

    def tile(v):
        return v.reshape(S5_KTILES, 1, S5_TILE_STATES)

    seg_mag = jnp.exp(float(S5_SEG) * a_re * dt)
    m_re = seg_mag * jnp.cos(float(S5_SEG) * a_im * dt)
    m_im = seg_mag * jnp.sin(float(S5_SEG) * a_im * dt)
    return wb, wc, tile(lam_re), tile(lam_im), tile(m_re), tile(m_im)


def _s5_layer(h, g_pre, g_post, params, d_skip, w_glu, b_glu):
    seq = h.shape[0]
    seg_spec = pl.BlockSpec((V7X_SUBLANES, S5_SEG * D_MODEL), lambda i: (i, 0))
    args = (h.reshape(seq // S5_SEG, S5_SEG * D_MODEL), g_pre.reshape(1, D_MODEL), g_post.reshape(1, D_MODEL),
            *params, d_skip.reshape(1, D_MODEL).astype(F32), w_glu.astype(BF16),
            b_glu.reshape(1, 2 * D_MODEL).astype(F32))
    in_specs = [seg_spec] + [_const_spec(a.shape) for a in args[1:]]
    out = pl.pallas_call(
        _s5_kernel,
        name="s5_mixer",
        grid=(seq // S5_ROWS,),
        in_specs=in_specs,
        out_specs=seg_spec,
        out_shape=jax.ShapeDtypeStruct((seq // S5_SEG, S5_SEG * D_MODEL), F32),
        scratch_shapes=[
            pltpu.VMEM((S5_ROWS, S5_TILE_STATES), F32),
            pltpu.VMEM((S5_ROWS, S5_TILE_STATES), F32),
            pltpu.VMEM((S5_ROWS, 2 * S5_TILE_STATES), BF16),
            pltpu.VMEM((S5_KTILES, 1, S5_TILE_STATES), F32),
            pltpu.VMEM((S5_KTILES, 1, S5_TILE_STATES), F32),
            pltpu.VMEM((S5_ROWS, D_MODEL), F32),
        ],
        compiler_params=pltpu.CompilerParams(
            dimension_semantics=("arbitrary",), vmem_limit_bytes=_vmem_limit(56 * 1024 * 1024)),
    )(*args)
    return out.reshape(seq, D_MODEL)


def _ffn_body(h, gpre_ref, gpost_ref, win_ref, wout_ref, o_ref):
    xn = _rms(h, gpre_ref[...]).astype(BF16)
    hd = _dot(xn, win_ref[...])
    a = (jax.nn.silu(hd[:, :D_FF]) * hd[:, D_FF:]).astype(BF16)
    f = _dot(a, wout_ref[...])
    o_ref[...] = h + _rms(f, gpost_ref[...])


def _ffn_kernel(h_ref, gpre_ref, gpost_ref, win_ref, wout_ref, o_ref):
    _ffn_body(h_ref[...], gpre_ref, gpost_ref, win_ref, wout_ref, o_ref)


def _oproj_ffn_kernel(h_ref, att_ref, wo_ref, gmix_ref, gpre_ref, gpost_ref, win_ref, wout_ref, o_ref):
    h = h_ref[...] + _rms(_dot(att_ref[...], wo_ref[...]), gmix_ref[...])
    _ffn_body(h, gpre_ref, gpost_ref, win_ref, wout_ref, o_ref)


def _ffn_layer(h, g_pre, g_post, w_in, w_out, att=None, w_o=None, g_mix=None):
    seq = h.shape[0]
    row_spec = pl.BlockSpec((FFN_ROWS, D_MODEL), lambda i: (i, 0))
    tail = (g_pre.reshape(1, D_MODEL), g_post.reshape(1, D_MODEL), w_in.astype(BF16), w_out.astype(BF16))
    if att is None:
        kern, args, specs = _ffn_kernel, (h,) + tail, [row_spec]
    else:
        head = (att, w_o.astype(BF16), g_mix.reshape(1, D_MODEL))
        kern, args = _oproj_ffn_kernel, (h,) + head + tail
        specs = [row_spec, row_spec, _const_spec(head[1].shape), _const_spec(head[2].shape)]
    specs = specs + [_const_spec(a.shape) for a in tail]
    return pl.pallas_call(
        kern,
        name="ffn",
        grid=(seq // FFN_ROWS,),
        in_specs=specs,
        out_specs=row_spec,
        out_shape=jax.ShapeDtypeStruct((seq, D_MODEL), F32),
        compiler_params=pltpu.CompilerParams(
            dimension_semantics=("arbitrary",), vmem_limit_bytes=_vmem_limit(56 * 1024 * 1024)),
    )(*args)


def _q_of(h, gq_ref, wq_ref):
    u = _rms(h, gq_ref[...]).astype(BF16)
    return (_dot(u, wq_ref[...]) * (HEAD_DIM ** -0.5 * LOG2_E)).astype(BF16)


def _qkv_kernel(h_ref, gq_ref, wq_ref, gkv_ref, wkt_ref, wv_ref, q_ref, kt_ref, v_ref):
    h = h_ref[...]
    q_ref[...] = _q_of(h, gq_ref, wq_ref)
    hk = _rms(h, gkv_ref[...]).astype(BF16)
    kt_ref[...] = lax.dot_general(wkt_ref[...], hk, (((1,), (1,)), ((), ())),
                                  preferred_element_type=F32).astype(BF16)
    v_ref[...] = _dot(hk, wv_ref[...]).astype(BF16)


def _q_kernel(h_ref, gq_ref, wq_ref, q_ref):
    q_ref[...] = _q_of(h_ref[...], gq_ref, wq_ref)


def _qkv_proj(h, g_q, w_q, g_kv, w_k, w_v):
    seq = h.shape[0]
    row_spec = pl.BlockSpec((PROJ_ROWS, D_MODEL), lambda i: (i, 0))
    col_spec = pl.BlockSpec((D_MODEL, PROJ_ROWS), lambda i: (0, i))
    args = (h, g_q.reshape(1, D_MODEL), w_q.astype(BF16), g_kv.reshape(1, D_MODEL),
            w_k.T.astype(BF16), w_v.astype(BF16))
    return pl.pallas_call(
        _qkv_kernel,
        name="qkv_proj",
        grid=(seq // PROJ_ROWS,),
        in_specs=[row_spec] + [_const_spec(a.shape) for a in args[1:]],
        out_specs=[row_spec, col_spec, row_spec],
        out_shape=[jax.ShapeDtypeStruct((seq, D_MODEL), BF16),
                   jax.ShapeDtypeStruct((D_MODEL, seq), BF16),
                   jax.ShapeDtypeStruct((seq, D_MODEL), BF16)],
        compiler_params=pltpu.CompilerParams(
            dimension_semantics=("arbitrary",), vmem_limit_bytes=_vmem_limit(40 * 1024 * 1024)),
    )(*args)


def _q_proj(h, g_q, w_q):
    seq = h.shape[0]
    row_spec = pl.BlockSpec((PROJ_ROWS, D_MODEL), lambda i: (i, 0))
    args = (h, g_q.reshape(1, D_MODEL), w_q.astype(BF16))
    return pl.pallas_call(
        _q_kernel,
        name="q_proj",
        grid=(seq // PROJ_ROWS,),
        in_specs=[row_spec] + [_const_spec(a.shape) for a in args[1:]],
        out_specs=row_spec,
        out_shape=jax.ShapeDtypeStruct((seq, D_MODEL), BF16),
        compiler_params=pltpu.CompilerParams(
            dimension_semantics=("arbitrary",), vmem_limit_bytes=_vmem_limit(40 * 1024 * 1024)),
    )(*args)


def _attn_kernel(q_ref, kt_ref, v_ref, o_ref, qm_ref, acc_ref, lr_ref):
    tile = pl.program_id(1)
    lane = lax.broadcasted_iota(jnp.int32, (ATT_KEYS, V7X_LANES), 1)
    head0 = lane < HEAD_DIM
    srow = lax.broadcasted_iota(jnp.int32, (ATT_PAIR_ROWS, ATT_KEYS), 0)
    skey = lax.broadcasted_iota(jnp.int32, (ATT_PAIR_ROWS, ATT_KEYS), 1)
    causal = jnp.where(srow >= ATT_KEYS, srow - ATT_KEYS, srow) > skey
    trow = lax.broadcasted_iota(jnp.int32, (2 * ATT_KEYS, 2 * ATT_KEYS), 0)
    tcol = lax.broadcasted_iota(jnp.int32, (2 * ATT_KEYS, 2 * ATT_KEYS), 1)
    tkey = jnp.where(trow >= ATT_KEYS, trow - ATT_KEYS, trow)
    tri_ones = jnp.where((tkey >= tcol) | (tcol >= ATT_KEYS), 1.0, 0.0).astype(BF16)

    for qb in range(ATT_QBLOCKS):
        q = q_ref[qb * ATT_KEYS:(qb + 1) * ATT_KEYS, :]
        zero = jnp.zeros_like(q)
        qm_ref[qb] = jnp.concatenate([jnp.where(head0, q, zero), jnp.where(head0, zero, q)], axis=0)

    def step(kv_blocks, keep, state):
        starts = [pl.multiple_of(kb * ATT_KEYS, ATT_KEYS) for kb in kv_blocks]
        zs = [_dot(qm_ref[qb], kt_ref[:, pl.ds(starts[qb], ATT_KEYS)]) for qb in range(ATT_QBLOCKS)]
        sps = []
        for qb in range(ATT_QBLOCKS):
            sp = jnp.maximum(zs[qb], 0.0) + jnp.log2(1.0 + jnp.exp2(_neg_abs(zs[qb])))
            sps.append(sp if keep is None else jnp.where(keep[qb], sp, 0.0))
        sp = jnp.concatenate(sps, axis=0)
        sp_hi = sp.astype(BF16)
        sp_lo = (sp - sp_hi.astype(F32)).astype(BF16)
        sums = _dot(jnp.concatenate([sp_hi, sp_lo], axis=1), tri_ones)
        out = []
        for qb in range(ATT_QBLOCKS):
            rows = slice(qb * ATT_PAIR_ROWS, (qb + 1) * ATT_PAIR_ROWS)
            logw = zs[qb] - sums[rows, :ATT_KEYS]
            total = sums[rows, ATT_KEYS:]
            if state is not None:
                acc, log_rest = state[qb]
                logw = logw + log_rest
            w = jnp.exp2(logw)
            if keep is not None:
                w = jnp.where(keep[qb], w, 0.0)
            pv = _dot(w.astype(BF16), v_ref[pl.ds(starts[qb], ATT_KEYS), :])
            out.append((pv, -total) if state is None else (acc + pv, log_rest - total))
        return out

    def load_state():
        return [(acc_ref[qb], lr_ref[qb]) for qb in range(ATT_QBLOCKS)]

    def store_state(state):
        top = None
        for qb, (acc, log_rest) in enumerate(state):
            acc_ref[qb] = acc
            lr_ref[qb] = log_rest
            top = log_rest if top is None else jnp.maximum(top, log_rest)
        return jnp.max(top)

    first_block = tile * ATT_QBLOCKS
    top = store_state(step([first_block + qb for qb in range(ATT_QBLOCKS)], [causal] * ATT_QBLOCKS, None))

    def walk_full(carry):
        s, _ = carry
        return s + 1, store_state(step([first_block + qb - s for qb in range(ATT_QBLOCKS)], None, load_state()))

    def walk_ragged(carry):
        s, _ = carry
        blocks = [first_block + qb - s for qb in range(ATT_QBLOCKS)]
        state = step([jnp.maximum(kb, 0) for kb in blocks], [kb >= 0 for kb in blocks], load_state())
        return s + 1, store_state(state)

    carry = lax.while_loop(lambda c: (c[0] <= first_block) & (c[1] > ATT_LOG2_CUTOFF), walk_full,
                           (jnp.int32(1), top))
    lax.while_loop(lambda c: (c[0] < first_block + ATT_QBLOCKS) & (c[1] > ATT_LOG2_CUTOFF), walk_ragged, carry)

    for qb in range(ATT_QBLOCKS):
        acc = acc_ref[qb]
        out = jnp.where(head0, acc[:ATT_KEYS], acc[ATT_KEYS:])
        o_ref[qb * ATT_KEYS:(qb + 1) * ATT_KEYS, :] = out.astype(BF16)


def _attention(q, kt, v):
    seq = q.shape[0]
    assert HEADS_PER_STEP == 2
    state = pltpu.VMEM((ATT_QBLOCKS, ATT_PAIR_ROWS, ATT_KEYS), F32)
    stacked_q = pltpu.VMEM((ATT_QBLOCKS, ATT_PAIR_ROWS, V7X_LANES), BF16)
    return pl.pallas_call(
        _attn_kernel,
        name="stickbreaking_attention",
        grid=(N_HEADS // HEADS_PER_STEP, seq // ATT_ROWS),
        in_specs=[pl.BlockSpec((ATT_ROWS, V7X_LANES), lambda p, i: (i, p)),
                  pl.BlockSpec((V7X_LANES, seq), lambda p, i: (p, 0)),
                  pl.BlockSpec((seq, V7X_LANES), lambda p, i: (0, p))],
        out_specs=pl.BlockSpec((ATT_ROWS, V7X_LANES), lambda p, i: (i, p)),
        out_shape=jax.ShapeDtypeStruct((seq, D_MODEL), BF16),
        scratch_shapes=[stacked_q, state, state],
        compiler_params=pltpu.CompilerParams(
            dimension_semantics=("arbitrary", "arbitrary"), vmem_limit_bytes=_vmem_limit(40 * 1024 * 1024)),
    )(q, kt, v)


def kernel(x, norm_mix_pre, norm_mix_post, norm_ffn_pre, norm_ffn_post, w_ffn_in, w_ffn_out, s5_log_dt, s5_a_re, s5_a_im, s5_b_re, s5_b_im, s5_c_re, s5_c_im, s5_d, s5_w_glu, s5_b_glu, kv_norm, w_k, w_v, w_q, w_o):
    bsz, seq, _ = x.shape
    assert bsz == 1 and seq % S5_ROWS == 0 and seq % ATT_ROWS == 0 and seq % FFN_ROWS == 0
    h = x.reshape(seq, D_MODEL)
    for layer in range(N_A_LAYERS):
        params = _s5_params(s5_log_dt[layer], s5_a_re[layer], s5_a_im[layer], s5_b_re[layer],
                            s5_b_im[layer], s5_c_re[layer], s5_c_im[layer])
        h = _s5_layer(h, norm_mix_pre[layer], norm_mix_post[layer], params, s5_d[layer],
                      s5_w_glu[layer], s5_b_glu[layer])
        h = _ffn_layer(h, norm_ffn_pre[layer], norm_ffn_post[layer], w_ffn_in[layer], w_ffn_out[layer])
    kt = v = None
    for layer in range(N_A_LAYERS, DEPTH):
        b = layer - N_A_LAYERS
        if b == 0:
            q, kt, v = _qkv_proj(h, norm_mix_pre[layer], w_q[b], kv_norm, w_k, w_v)
        else:
            q = _q_proj(h, norm_mix_pre[layer], w_q[b])
        att = _attention(q, kt, v)
        h = _ffn_layer(h, norm_ffn_pre[layer], norm_ffn_post[layer], w_ffn_in[layer], w_ffn_out[layer],
                       att=att, w_o=w_o[b], g_mix=norm_mix_post[layer])
    return h.reshape(bsz, seq, D_MODEL)
```

```python
import functools
import math

import jax
import jax.numpy as jnp
from jax import lax
from jax.experimental import pallas as pl
from jax.experimental.pallas import tpu as pltpu

F32 = jnp.float32
BF16 = jnp.bfloat16

D_MODEL = 1024
DEPTH = 4
N_A_LAYERS = DEPTH // 2
SSM_GROUP = 16
SSM_GROUPS = D_MODEL // SSM_GROUP
SSM_STATE = 64
N_HEADS = 16
HEAD_DIM = D_MODEL // N_HEADS
D_FF = 2816
EPS = 1e-6

V7X_LANES = 128
V7X_SUBLANES = 8
BF16_ROWS = 2 * V7X_SUBLANES
V7X_MXU_DIM = 256
V7X_VMEM_BYTES = 64 * 1024 * 1024

S5_KTILES = D_MODEL // V7X_MXU_DIM
S5_TILE_GROUPS = V7X_MXU_DIM // SSM_GROUP
S5_TILE_STATES = S5_TILE_GROUPS * SSM_STATE
S5_SEG = 64
S5_ROWS = V7X_SUBLANES * S5_SEG
S5_COLS = 512

FFN_ROWS = 512
PROJ_ROWS = 512

ATT_KEYS = 128
ATT_QBLOCKS = 8
ATT_ROWS = ATT_QBLOCKS * ATT_KEYS
HEADS_PER_STEP = V7X_LANES // HEAD_DIM
ATT_PAIR_ROWS = HEADS_PER_STEP * ATT_KEYS
LOG2_E = math.log2(math.e)
ATT_LOG2_CUTOFF = -104.0 * LOG2_E


def _vmem_limit(nbytes):
    return int(min(nbytes, V7X_VMEM_BYTES - 6 * 1024 * 1024))


def _rms(x, g):
    return x * lax.rsqrt(jnp.mean(x * x, axis=-1, keepdims=True) + EPS) * g


def _dot(a, b):
    return jnp.dot(a, b, preferred_element_type=F32)


def _neg_abs(x):
    return -jnp.abs(x)


def _const_spec(shape):
    zeros = (0,) * len(shape)
    return pl.BlockSpec(shape, lambda *_: zeros, pipeline_mode=pl.Buffered(1))


def _s5_kernel(h_ref, gpre_ref, gpost_ref, wb_ref, wc_ref, lre_ref, lim_ref, mre_ref, mim_ref,
               d_ref, wglu_ref, bglu_ref, o_ref,
               xre_ref, xim_ref, xb_ref, cre_ref, cim_ref, y_ref):
    @pl.when(pl.program_id(0) == 0)
    def _():
        cre_ref[...] = jnp.zeros_like(cre_ref)
        cim_ref[...] = jnp.zeros_like(cim_ref)

    h = h_ref[...]
    u = _rms(h, gpre_ref[...])
    u_bf = u.astype(BF16)
    zeros = jnp.zeros((V7X_SUBLANES, S5_COLS), F32)
    row = lax.broadcasted_iota(jnp.int32, (V7X_SUBLANES, S5_COLS), 0)

    for kt in range(S5_KTILES):
        bu = _dot(u_bf[:, kt * V7X_MXU_DIM:(kt + 1) * V7X_MXU_DIM], wb_ref[kt])
        xre_ref[...] = bu[:, :S5_TILE_STATES]
        xim_ref[...] = bu[:, S5_TILE_STATES:]

        for cg in range(S5_TILE_STATES // S5_COLS):
            cs = slice(cg * S5_COLS, (cg + 1) * S5_COLS)
            lre = jnp.broadcast_to(lre_ref[kt, :, cs], (V7X_SUBLANES, S5_COLS))
            lim = jnp.broadcast_to(lim_ref[kt, :, cs], (V7X_SUBLANES, S5_COLS))

            def local_scan(i, carry):
                xr, xi = carry
                r0 = pl.multiple_of(i * V7X_SUBLANES, V7X_SUBLANES)
                nr = lre * xr - lim * xi + xre_ref[pl.ds(r0, V7X_SUBLANES), cs]
                ni = lre * xi + lim * xr + xim_ref[pl.ds(r0, V7X_SUBLANES), cs]
                xre_ref[pl.ds(r0, V7X_SUBLANES), cs] = nr
                xim_ref[pl.ds(r0, V7X_SUBLANES), cs] = ni
                return nr, ni

            er, ei = lax.fori_loop(0, S5_SEG, local_scan, (zeros, zeros), unroll=4)

            mre = mre_ref[kt, :, cs]
            mim = mim_ref[kt, :, cs]
            cur_r = cre_ref[kt, :, cs]
            cur_i = cim_ref[kt, :, cs]
            xin_r = zeros
            xin_i = zeros
            for r in range(V7X_SUBLANES):
                xin_r = jnp.where(row == r, cur_r, xin_r)
                xin_i = jnp.where(row == r, cur_i, xin_i)
                nxt_r = mre * cur_r - mim * cur_i + er[r:r + 1, :]
                nxt_i = mre * cur_i + mim * cur_r + ei[r:r + 1, :]
                cur_r, cur_i = nxt_r, nxt_i
            cre_ref[kt, :, cs] = cur_r
            cim_ref[kt, :, cs] = cur_i

            cs_im = slice(S5_TILE_STATES + cg * S5_COLS, S5_TILE_STATES + (cg + 1) * S5_COLS)

            def add_incoming(j, carry):
                cr, ci = carry
                r0 = pl.multiple_of(j * BF16_ROWS, BF16_ROWS)
                full_r, full_i = [], []
                for half in range(BF16_ROWS // V7X_SUBLANES):
                    cr, ci = lre * cr - lim * ci, lre * ci + lim * cr
                    rows = pl.ds(r0 + half * V7X_SUBLANES, V7X_SUBLANES)
                    full_r.append(xre_ref[rows, cs] + cr)
                    full_i.append(xim_ref[rows, cs] + ci)
                xb_ref[pl.ds(r0, BF16_ROWS), cs] = jnp.concatenate(full_r, axis=0).astype(BF16)
                xb_ref[pl.ds(r0, BF16_ROWS), cs_im] = jnp.concatenate(full_i, axis=0).astype(BF16)
                return cr, ci

            lax.fori_loop(0, S5_ROWS // BF16_ROWS, add_incoming, (xin_r, xin_i), unroll=2)

        y_ref[:, kt * V7X_MXU_DIM:(kt + 1) * V7X_MXU_DIM] = _dot(xb_ref[...], wc_ref[kt])

    y = y_ref[...] + d_ref[...] * u
    a = jax.nn.gelu(y).astype(BF16)
    z = _dot(a, wglu_ref[...]) + bglu_ref[...]
    mix = z[:, :D_MODEL] * jax.nn.sigmoid(z[:, D_MODEL:])
    o_ref[...] = h + _rms(mix, gpost_ref[...])


def _s5_params(log_dt, a_re, a_im, b_re, b_im, c_re, c_im):
    dt = jnp.exp(log_dt.astype(F32))[:, None]
    a_re = a_re.astype(F32)
    a_im = a_im.astype(F32)
    mag = jnp.exp(a_re * dt)
    lam_re = mag * jnp.cos(a_im * dt)
    lam_im = mag * jnp.sin(a_im * dt)
    den = a_re * a_re + a_im * a_im
    nr = lam_re - 1.0
    coef_re = (nr * a_re + lam_im * a_im) / den
    coef_im = (lam_im * a_re - nr * a_im) / den
    bb_re = coef_re[..., None] * b_re - coef_im[..., None] * b_im
    bb_im = coef_re[..., None] * b_im + coef_im[..., None] * b_re

    eye = jnp.eye(S5_TILE_GROUPS, dtype=F32)

    def in_proj(b):
        b = b.reshape(S5_KTILES, S5_TILE_GROUPS, SSM_STATE, SSM_GROUP)
        w = jnp.einsum('kgpc,gh->kgchp', b, eye)
        return w.reshape(S5_KTILES, V7X_MXU_DIM, S5_TILE_STATES)

    def out_proj(c):
        c = c.astype(F32).reshape(S5_KTILES, S5_TILE_GROUPS, SSM_GROUP, SSM_STATE)
        w = jnp.einsum('kgcp,gh->kgphc', c, eye)
        return w.reshape(S5_KTILES, S5_TILE_STATES, V7X_MXU_DIM)

    wb = jnp.concatenate([in_proj(bb_re), in_proj(bb_im)], axis=2).astype(BF16)
    wc = jnp.concatenate([out_proj(c_re), -out_proj(c_im)], axis=1).astype(BF16)

    def tile(v):
        return v.reshape(S5_KTILES, 1, S5_TILE_STATES)

    seg_mag = jnp.exp(float(S5_SEG) * a_re * dt)
    m_re = seg_mag * jnp.cos(float(S5_SEG) * a_im * dt)
    m_im = seg_mag * jnp.sin(float(S5_SEG) * a_im * dt)
    return wb, wc, tile(lam_re), tile(lam_im), tile(m_re), tile(m_im)


def _s5_layer(h, g_pre, g_post, params, d_skip, w_glu, b_glu):
    seq = h.shape[0]
    row_spec = pl.BlockSpec((S5_ROWS, D_MODEL), lambda i: (i, 0))
    args = (h, g_pre.reshape(1, D_MODEL), g_post.reshape(1, D_MODEL),
            *params, d_skip.reshape(1, D_MODEL).astype(F32), w_glu.astype(BF16),
            b_glu.reshape(1, 2 * D_MODEL).astype(F32))
    in_specs = [row_spec] + [_const_spec(a.shape) for a in args[1:]]
    return pl.pallas_call(
        _s5_kernel,
        name="s5_mixer",
        grid=(seq // S5_ROWS,),
        in_specs=in_specs,
        out_specs=row_spec,
        out_shape=jax.ShapeDtypeStruct((seq, D_MODEL), F32),
        scratch_shapes=[
            pltpu.VMEM((S5_ROWS, S5_TILE_STATES), F32),
            pltpu.VMEM((S5_ROWS, S5_TILE_STATES), F32),
            pltpu.VMEM((S5_ROWS, 2 * S5_TILE_STATES), BF16),
            pltpu.VMEM((S5_KTILES, 1, S5_TILE_STATES), F32),
            pltpu.VMEM((S5_KTILES, 1, S5_TILE_STATES), F32),
            pltpu.VMEM((S5_ROWS, D_MODEL), F32),
        ],
        compiler_params=pltpu.CompilerParams(
            dimension_semantics=("arbitrary",), vmem_limit_bytes=_vmem_limit(56 * 1024 * 1024)),
    )(*args)


def _ffn_body(h, gpre_ref, gpost_ref, win_ref, wout_ref, o_ref):
    xn = _rms(h, gpre_ref[...]).astype(BF16)
    hd = _dot(xn, win_ref[...])
    a = (jax.nn.silu(hd[:, :D_FF]) * hd[:, D_FF:]).astype(BF16)
    f = _dot(a, wout_ref[...])
    o_ref[...] = h + _rms(f, gpost_ref[...])


def _ffn_kernel(h_ref, gpre_ref, gpost_ref, win_ref, wout_ref, o_ref):
    _ffn_body(h_ref[...], gpre_ref, gpost_ref, win_ref, wout_ref, o_ref)


def _oproj_ffn_kernel(h_ref, att_ref, wo_ref, gmix_ref, gpre_ref, gpost_ref, win_ref, wout_ref, o_ref):
    h = h_ref[...] + _rms(_dot(att_ref[...], wo_ref[...]), gmix_ref[...])
    _ffn_body(h, gpre_ref, gpost_ref, win_ref, wout_ref, o_ref)


def _ffn_layer(h, g_pre, g_post, w_in, w_out, att=None, w_o=None, g_mix=None):
    seq = h.shape[0]
    row_spec = pl.BlockSpec((FFN_ROWS, D_MODEL), lambda i: (i, 0))
    tail = (g_pre.reshape(1, D_MODEL), g_post.reshape(1, D_MODEL), w_in.astype(BF16), w_out.astype(BF16))
    if att is None:
        kern, args, specs = _ffn_kernel, (h,) + tail, [row_spec]
    else:
        head = (att, w_o.astype(BF16), g_mix.reshape(1, D_MODEL))
        kern, args = _oproj_ffn_kernel, (h,) + head + tail
        specs = [row_spec, row_spec, _const_spec(head[1].shape), _const_spec(head[2].shape)]
    specs = specs + [_const_spec(a.shape) for a in tail]
    return pl.pallas_call(
        kern,
        name="ffn",
        grid=(seq // FFN_ROWS,),
        in_specs=specs,
        out_specs=row_spec,
        out_shape=jax.ShapeDtypeStruct((seq, D_MODEL), F32),
        compiler_params=pltpu.CompilerParams(
            dimension_semantics=("arbitrary",), vmem_limit_bytes=_vmem_limit(56 * 1024 * 1024)),
    )(*args)


def _q_of(h, gq_ref, wq_ref):
    u = _rms(h, gq_ref[...]).astype(BF16)
    return (_dot(u, wq_ref[...]) * (HEAD_DIM ** -0.5 * LOG2_E)).astype(BF16)


def _qkv_kernel(h_ref, gq_ref, wq_ref, gkv_ref, wkt_ref, wv_ref, q_ref, kt_ref, v_ref):
    h = h_ref[...]
    q_ref[...] = _q_of(h, gq_ref, wq_ref)
    hk = _rms(h, gkv_ref[...]).astype(BF16)
    kt_ref[...] = lax.dot_general(wkt_ref[...], hk, (((1,), (1,)), ((), ())),
                                  preferred_element_type=F32).astype(BF16)
    v_ref[...] = _dot(hk, wv_ref[...]).astype(BF16)


def _q_kernel(h_ref, gq_ref, wq_ref, q_ref):
    q_ref[...] = _q_of(h_ref[...], gq_ref, wq_ref)


def _qkv_proj(h, g_q, w_q, g_kv, w_k, w_v):
    seq = h.shape[0]
    row_spec = pl.BlockSpec((PROJ_ROWS, D_MODEL), lambda i: (i, 0))
    col_spec = pl.BlockSpec((D_MODEL, PROJ_ROWS), lambda i: (0, i))
    args = (h, g_q.reshape(1, D_MODEL), w_q.astype(BF16), g_kv.reshape(1, D_MODEL),
            w_k.T.astype(BF16), w_v.astype(BF16))
    return pl.pallas_call(
        _qkv_kernel,
        name="qkv_proj",
        grid=(seq // PROJ_ROWS,),
        in_specs=[row_spec] + [_const_spec(a.shape) for a in args[1:]],
        out_specs=[row_spec, col_spec, row_spec],
        out_shape=[jax.ShapeDtypeStruct((seq, D_MODEL), BF16),
                   jax.ShapeDtypeStruct((D_MODEL, seq), BF16),
                   jax.ShapeDtypeStruct((seq, D_MODEL), BF16)],
        compiler_params=pltpu.CompilerParams(
            dimension_semantics=("arbitrary",), vmem_limit_bytes=_vmem_limit(40 * 1024 * 1024)),
    )(*args)


def _q_proj(h, g_q, w_q):
    seq = h.shape[0]
    row_spec = pl.BlockSpec((PROJ_ROWS, D_MODEL), lambda i: (i, 0))
    args = (h, g_q.reshape(1, D_MODEL), w_q.astype(BF16))
    return pl.pallas_call(
        _q_kernel,
        name="q_proj",
        grid=(seq // PROJ_ROWS,),
        in_specs=[row_spec] + [_const_spec(a.shape) for a in args[1:]],
        out_specs=row_spec,
        out_shape=jax.ShapeDtypeStruct((seq, D_MODEL), BF16),
        compiler_params=pltpu.CompilerParams(
            dimension_semantics=("arbitrary",), vmem_limit_bytes=_vmem_limit(40 * 1024 * 1024)),
    )(*args)


def _attn_kernel(q_ref, kt_ref, v_ref, o_ref, qm_ref, acc_ref, lr_ref):
    tile = pl.program_id(1)
    lane = lax.broadcasted_iota(jnp.int32, (ATT_KEYS, V7X_LANES), 1)
    head0 = lane < HEAD_DIM
    srow = lax.broadcasted_iota(jnp.int32, (ATT_PAIR_ROWS, ATT_KEYS), 0)
    skey = lax.broadcasted_iota(jnp.int32, (ATT_PAIR_ROWS, ATT_KEYS), 1)
    causal = jnp.where(srow >= ATT_KEYS, srow - ATT_KEYS, srow) > skey
    trow = lax.broadcasted_iota(jnp.int32, (2 * ATT_KEYS, 2 * ATT_KEYS), 0)
    tcol = lax.broadcasted_iota(jnp.int32, (2 * ATT_KEYS, 2 * ATT_KEYS), 1)
    tkey = jnp.where(trow >= ATT_KEYS, trow - ATT_KEYS, trow)
    tri_ones = jnp.where((tkey >= tcol) | (tcol >= ATT_KEYS), 1.0, 0.0).astype(BF16)

    for qb in range(ATT_QBLOCKS):
        q = q_ref[qb * ATT_KEYS:(qb + 1) * ATT_KEYS, :]
        zero = jnp.zeros_like(q)
        qm_ref[qb] = jnp.concatenate([jnp.where(head0, q, zero), jnp.where(head0, zero, q)], axis=0)

    half = ATT_KEYS // 2
    early = [slice(hh * ATT_KEYS, hh * ATT_KEYS + half) for hh in range(HEADS_PER_STEP)]
    late = [slice(hh * ATT_KEYS + half, (hh + 1) * ATT_KEYS) for hh in range(HEADS_PER_STEP)]

    def pick(ref, qb, parts):
        return ref[qb] if parts is None else jnp.concatenate([ref[qb, p, :] for p in parts], axis=0)

    def step(kv_blocks, keep, state, parts=None):
        nrows = ATT_PAIR_ROWS if parts is None else ATT_PAIR_ROWS // 2
        starts = [pl.multiple_of(kb * ATT_KEYS, ATT_KEYS) for kb in kv_blocks]
        zs = [_dot(pick(qm_ref, qb, parts), kt_ref[:, pl.ds(starts[qb], ATT_KEYS)]) for qb in range(ATT_QBLOCKS)]
        sps = []
        for qb in range(ATT_QBLOCKS):
            sp = jnp.maximum(zs[qb], 0.0) + jnp.log2(1.0 + jnp.exp2(_neg_abs(zs[qb])))
            sps.append(sp if keep is None else jnp.where(keep[qb], sp, 0.0))
        sp = jnp.concatenate(sps, axis=0)
        sp_hi = sp.astype(BF16)
        sp_lo = (sp - sp_hi.astype(F32)).astype(BF16)
        sums = _dot(jnp.concatenate([sp_hi, sp_lo], axis=1), tri_ones)
        out = []
        for qb in range(ATT_QBLOCKS):
            rows = slice(qb * nrows, (qb + 1) * nrows)
            logw = zs[qb] - sums[rows, :ATT_KEYS]
            total = sums[rows, ATT_KEYS:]
            if state is not None:
                acc, log_rest = state[qb]
                logw = logw + log_rest
            w = jnp.exp2(logw)
            if keep is not None:
                w = jnp.where(keep[qb], w, 0.0)
            pv = _dot(w.astype(BF16), v_ref[pl.ds(starts[qb], ATT_KEYS), :])
            out.append((pv, -total) if state is None else (acc + pv, log_rest - total))
        return out

    def load_state(parts=None):
        return [(pick(acc_ref, qb, parts), pick(lr_ref, qb, parts)) for qb in range(ATT_QBLOCKS)]

    def store_state(state):
        top = None
        for qb, (acc, log_rest) in enumerate(state):
            acc_ref[qb] = acc
            lr_ref[qb] = log_rest
            top = log_rest if top is None else jnp.maximum(top, log_rest)
        top_early = jnp.max(jnp.maximum(top[early[0]], top[early[1]]))
        top_late = jnp.max(jnp.maximum(top[late[0]], top[late[1]]))
        return top_early, top_late

    def store_early(state):
        top = None
        for qb, (acc, log_rest) in enumerate(state):
            for n, p in enumerate(early):
                acc_ref[qb, p, :] = acc[n * half:(n + 1) * half]
                lr_ref[qb, p, :] = log_rest[n * half:(n + 1) * half]
            top = log_rest if top is None else jnp.maximum(top, log_rest)
        return jnp.max(top)

    first_block = tile * ATT_QBLOCKS
    tops = store_state(step([first_block + qb for qb in range(ATT_QBLOCKS)], [causal] * ATT_QBLOCKS, None))

    def walk_full(carry):
        s = carry[0]
        return (s + 1,) + store_state(step([first_block + qb - s for qb in range(ATT_QBLOCKS)], None, load_state()))

    def walk_early(carry):
        s, _, top_late = carry
        state = step([first_block + qb - s for qb in range(ATT_QBLOCKS)], None, load_state(early), early)
        return s + 1, store_early(state), top_late

    def walk_ragged(carry):
        s = carry[0]
        blocks = [first_block + qb - s for qb in range(ATT_QBLOCKS)]
        state = step([jnp.maximum(kb, 0) for kb in blocks], [kb >= 0 for kb in blocks], load_state())
        return (s + 1,) + store_state(state)

    carry = (jnp.int32(1),) + tops
    carry = lax.while_loop(lambda c: (c[0] <= first_block) & (c[2] > ATT_LOG2_CUTOFF), walk_full, carry)
    carry = lax.while_loop(lambda c: (c[0] <= first_block) & (c[1] > ATT_LOG2_CUTOFF), walk_early, carry)
    lax.while_loop(lambda c: (c[0] < first_block + ATT_QBLOCKS) & (jnp.maximum(c[1], c[2]) > ATT_LOG2_CUTOFF),
                   walk_ragged, carry)

    for qb in range(ATT_QBLOCKS):
        acc = acc_ref[qb]
        out = jnp.where(head0, acc[:ATT_KEYS], acc[ATT_KEYS:])
        o_ref[qb * ATT_KEYS:(qb + 1) * ATT_KEYS, :] = out.astype(BF16)


def _attention(q, kt, v):
    seq = q.shape[0]
    assert HEADS_PER_STEP == 2
    state = pltpu.VMEM((ATT_QBLOCKS, ATT_PAIR_ROWS, ATT_KEYS), F32)
    stacked_q = pltpu.VMEM((ATT_QBLOCKS, ATT_PAIR_ROWS, V7X_LANES), BF16)
    return pl.pallas_call(
        _attn_kernel,
        name="stickbreaking_attention",
        grid=(N_HEADS // HEADS_PER_STEP, seq // ATT_ROWS),
        in_specs=[pl.BlockSpec((ATT_ROWS, V7X_LANES), lambda p, i: (i, p)),
                  pl.BlockSpec((V7X_LANES, seq), lambda p, i: (p, 0)),
                  pl.BlockSpec((seq, V7X_LANES), lambda p, i: (0, p))],
        out_specs=pl.BlockSpec((ATT_ROWS, V7X_LANES), lambda p, i: (i, p)),
        out_shape=jax.ShapeDtypeStruct((seq, D_MODEL), BF16),
        scratch_shapes=[stacked_q, state, state],
        compiler_params=pltpu.CompilerParams(
            dimension_semantics=("arbitrary", "arbitrary"), vmem_limit_bytes=_vmem_limit(40 * 1024 * 1024)),
    )(q, kt, v)


def kernel(x, norm_mix_pre, norm_mix_post, norm_ffn_pre, norm_ffn_post, w_ffn_in, w_ffn_out, s5_log_dt, s5_a_re, s5_a_im, s5_b_re, s5_b_im, s5_c_re, s5_c_im, s5_d, s5_w_glu, s5_b_glu, kv_norm, w_k, w_v, w_q, w_o):
    bsz, seq, _ = x.shape
    assert bsz == 1 and seq % S5_ROWS == 0 and seq % ATT_ROWS == 0 and seq % FFN_ROWS == 0
    blocks = seq // S5_ROWS
    h = x.reshape(blocks, V7X_SUBLANES, S5_SEG, D_MODEL).transpose(0, 2, 1, 3).reshape(seq, D_MODEL)
    for layer in range(N_A_LAYERS):
        params = _s5_params(s5_log_dt[layer], s5_a_re[layer], s5_a_im[layer], s5_b_re[layer],
                            s5_b_im[layer], s5_c_re[layer], s5_c_im[layer])
        h = _s5_layer(h, norm_mix_pre[layer], norm_mix_post[layer], params, s5_d[layer],
                      s5_w_glu[layer], s5_b_glu[layer])
        h = _ffn_layer(h, norm_ffn_pre[layer], norm_ffn_post[layer], w_ffn_in[layer], w_ffn_out[layer])
    h = h.reshape(blocks, S5_SEG, V7X_SUBLANES, D_MODEL).transpose(0, 2, 1, 3).reshape(seq, D_MODEL)

    kt = v = None
    for layer in range(N_A_LAYERS, DEPTH):
        b = layer - N_A_LAYERS
        if b == 0:
            q, kt, v = _qkv_proj(h, norm_mix_pre[layer], w_q[b], kv_norm, w_k, w_v)
        else:
            q = _q_proj(h, norm_mix_pre[layer], w_q[b])
        att = _attention(q, kt, v)
        h = _ffn_layer(h, norm_ffn_pre[layer], norm_ffn_post[layer], w_ffn_in[layer], w_ffn_out[layer],
                       att=att, w_o=w_o[b], g_mix=norm_mix_post[layer])
    return h.reshape(bsz, seq, D_MODEL)
```

```python
import functools
import math

import jax
import jax.numpy as jnp
from jax import lax
from jax.experimental import pallas as pl
from jax.experimental.pallas import tpu as pltpu

F32 = jnp.float32
BF16 = jnp.bfloat16

D_MODEL = 1024
DEPTH = 4
N_A_LAYERS = DEPTH // 2
SSM_GROUP = 16
SSM_GROUPS = D_MODEL // SSM_GROUP
SSM_STATE = 64
N_HEADS = 16
HEAD_DIM = D_MODEL // N_HEADS
D_FF = 2816
EPS = 1e-6

V7X_LANES = 128
V7X_SUBLANES = 8
BF16_ROWS = 2 * V7X_SUBLANES
V7X_MXU_DIM = 256
V7X_VMEM_BYTES = 64 * 1024 * 1024

S5_KTILES = D_MODEL // V7X_MXU_DIM
S5_TILE_GROUPS = V7X_MXU_DIM // SSM_GROUP
S5_TILE_STATES = S5_TILE_GROUPS * SSM_STATE
S5_SEG = 64
S5_ROWS = V7X_SUBLANES * S5_SEG
S5_COLS = 512

FFN_ROWS = 512
PROJ_ROWS = 512

ATT_KEYS = 128
ATT_QBLOCKS = 8
ATT_ROWS = ATT_QBLOCKS * ATT_KEYS
HEADS_PER_STEP = V7X_LANES // HEAD_DIM
ATT_PAIR_ROWS = HEADS_PER_STEP * ATT_KEYS
LOG2_E = math.log2(math.e)
ATT_LOG2_CUTOFF = -104.0 * LOG2_E


def _vmem_limit(nbytes):
    return int(min(nbytes, V7X_VMEM_BYTES - 6 * 1024 * 1024))


def _rms(x, g):
    return x * lax.rsqrt(jnp.mean(x * x, axis=-1, keepdims=True) + EPS) * g


def _dot(a, b):
    return jnp.dot(a, b, preferred_element_type=F32)


def _neg_abs(x):
    return -jnp.abs(x)


def _const_spec(shape):
    zeros = (0,) * len(shape)
    return pl.BlockSpec(shape, lambda *_: zeros, pipeline_mode=pl.Buffered(1))


def _s5_kernel(h_ref, gpre_ref, gpost_ref, wb_ref, wc_ref, lre_ref, lim_ref, mre_ref, mim_ref,
               d_ref, wglu_ref, bglu_ref, o_ref,
               xre_ref, xim_ref, xb_ref, cre_ref, cim_ref, y_ref):
    @pl.when(pl.program_id(0) == 0)
    def _():
        cre_ref[...] = jnp.zeros_like(cre_ref)
        cim_ref[...] = jnp.zeros_like(cim_ref)

    h = h_ref[...]
    u = _rms(h, gpre_ref[...])
    u_bf = u.astype(BF16)
    zeros = jnp.zeros((V7X_SUBLANES, S5_COLS), F32)
    row = lax.broadcasted_iota(jnp.int32, (V7X_SUBLANES, S5_COLS), 0)

    for kt in range(S5_KTILES):
        slot = kt % 2
        bu = _dot(u_bf[:, kt * V7X_MXU_DIM:(kt + 1) * V7X_MXU_DIM], wb_ref[kt])
        xre_ref[slot] = bu[:, :S5_TILE_STATES]
        xim_ref[slot] = bu[:, S5_TILE_STATES:]

        for cg in range(S5_TILE_STATES // S5_COLS):
            cs = slice(cg * S5_COLS, (cg + 1) * S5_COLS)
            lre = jnp.broadcast_to(lre_ref[kt, :, cs], (V7X_SUBLANES, S5_COLS))
            lim = jnp.broadcast_to(lim_ref[kt, :, cs], (V7X_SUBLANES, S5_COLS))

            er, ei = zeros, zeros
            for i in range(S5_SEG):
                rows = pl.ds(i * V7X_SUBLANES, V7X_SUBLANES)
                er, ei = (lre * er - lim * ei + xre_ref[slot, rows, cs],
                          lre * ei + lim * er + xim_ref[slot, rows, cs])
                xre_ref[slot, rows, cs] = er
                xim_ref[slot, rows, cs] = ei

            mre = mre_ref[kt, :, cs]
            mim = mim_ref[kt, :, cs]
            cur_r = cre_ref[kt, :, cs]
            cur_i = cim_ref[kt, :, cs]
            xin_r = zeros
            xin_i = zeros
            for r in range(V7X_SUBLANES):
                xin_r = jnp.where(row == r, cur_r, xin_r)
                xin_i = jnp.where(row == r, cur_i, xin_i)
                nxt_r = mre * cur_r - mim * cur_i + er[r:r + 1, :]
                nxt_i = mre * cur_i + mim * cur_r + ei[r:r + 1, :]
                cur_r, cur_i = nxt_r, nxt_i
            cre_ref[kt, :, cs] = cur_r
            cim_ref[kt, :, cs] = cur_i

            cs_im = slice(S5_TILE_STATES + cg * S5_COLS, S5_TILE_STATES + (cg + 1) * S5_COLS)

            cr, ci = xin_r, xin_i
            for j in range(S5_ROWS // BF16_ROWS):
                full_r, full_i = [], []
                for half in range(BF16_ROWS // V7X_SUBLANES):
                    cr, ci = lre * cr - lim * ci, lre * ci + lim * cr
                    rows = pl.ds(j * BF16_ROWS + half * V7X_SUBLANES, V7X_SUBLANES)
                    full_r.append(xre_ref[slot, rows, cs] + cr)
                    full_i.append(xim_ref[slot, rows, cs] + ci)
                packed = pl.ds(j * BF16_ROWS, BF16_ROWS)
                xb_ref[slot, packed, cs] = jnp.concatenate(full_r, axis=0).astype(BF16)
                xb_ref[slot, packed, cs_im] = jnp.concatenate(full_i, axis=0).astype(BF16)

        y_ref[:, kt * V7X_MXU_DIM:(kt + 1) * V7X_MXU_DIM] = _dot(xb_ref[slot], wc_ref[kt])

    y = y_ref[...] + d_ref[...] * u
    a = jax.nn.gelu(y).astype(BF16)
    z = _dot(a, wglu_ref[...]) + bglu_ref[...]
    mix = z[:, :D_MODEL] * jax.nn.sigmoid(z[:, D_MODEL:])
    o_ref[...] = h + _rms(mix, gpost_ref[...])


def _s5_params(log_dt, a_re, a_im, b_re, b_im, c_re, c_im):
    dt = jnp.exp(log_dt.astype(F32))[:, None]
    a_re = a_re.astype(F32)
    a_im = a_im.astype(F32)
    mag = jnp.exp(a_re * dt)
    lam_re = mag * jnp.cos(a_im * dt)
    lam_im = mag * jnp.sin(a_im * dt)
    den = a_re * a_re + a_im * a_im
    nr = lam_re - 1.0
    coef_re = (nr * a_re + lam_im * a_im) / den
    coef_im = (lam_im * a_re - nr * a_im) / den
    bb_re = coef_re[..., None] * b_re - coef_im[..., None] * b_im
    bb_im = coef_re[..., None] * b_im + coef_im[..., None] * b_re

    eye = jnp.eye(S5_TILE_GROUPS, dtype=F32)

    def in_proj(b):
        b = b.reshape(S5_KTILES, S5_TILE_GROUPS, SSM_STATE, SSM_GROUP)
        w = jnp.einsum('kgpc,gh->kgchp', b, eye)
        return w.reshape(S5_KTILES, V7X_MXU_DIM, S5_TILE_STATES)

    def out_proj(c):
        c = c.astype(F32).reshape(S5_KTILES, S5_TILE_GROUPS, SSM_GROUP, SSM_STATE)
        w = jnp.einsum('kgcp,gh->kgphc', c, eye)
        return w.reshape(S5_KTILES, S5_TILE_STATES, V7X_MXU_DIM)

    wb = jnp.concatenate([in_proj(bb_re), in_proj(bb_im)], axis=2).astype(BF16)
    wc = jnp.concatenate([out_proj(c_re), -out_proj(c_im)], axis=1).astype(BF16)

    def tile(v):
        return v.reshape(S5_KTILES, 1, S5_TILE_STATES)

    seg_mag = jnp.exp(float(S5_SEG) * a_re * dt)
    m_re = seg_mag * jnp.cos(float(S5_SEG) * a_im * dt)
    m_im = seg_mag * jnp.sin(float(S5_SEG) * a_im * dt)
    return wb, wc, tile(lam_re), tile(lam_im), tile(m_re), tile(m_im)


def _s5_layer(h, g_pre, g_post, params, d_skip, w_glu, b_glu):
    seq = h.shape[0]
    row_spec = pl.BlockSpec((S5_ROWS, D_MODEL), lambda i: (i, 0))
    args = (h, g_pre.reshape(1, D_MODEL), g_post.reshape(1, D_MODEL),
            *params, d_skip.reshape(1, D_MODEL).astype(F32), w_glu.astype(BF16),
            b_glu.reshape(1, 2 * D_MODEL).astype(F32))
    in_specs = [row_spec] + [_const_spec(a.shape) for a in args[1:]]
    return pl.pallas_call(
        _s5_kernel,
        name="s5_mixer",
        grid=(seq // S5_ROWS,),
        in_specs=in_specs,
        out_specs=row_spec,
        out_shape=jax.ShapeDtypeStruct((seq, D_MODEL), F32),
        scratch_shapes=[
            pltpu.VMEM((2, S5_ROWS, S5_TILE_STATES), F32),
            pltpu.VMEM((2, S5_ROWS, S5_TILE_STATES), F32),
            pltpu.VMEM((2, S5_ROWS, 2 * S5_TILE_STATES), BF16),
            pltpu.VMEM((S5_KTILES, 1, S5_TILE_STATES), F32),
            pltpu.VMEM((S5_KTILES, 1, S5_TILE_STATES), F32),
            pltpu.VMEM((S5_ROWS, D_MODEL), F32),
        ],
        compiler_params=pltpu.CompilerParams(
            dimension_semantics=("arbitrary",), vmem_limit_bytes=_vmem_limit(56 * 1024 * 1024)),
    )(*args)


def _ffn_body(h, gpre_ref, gpost_ref, win_ref, wout_ref, o_ref):
    xn = _rms(h, gpre_ref[...]).astype(BF16)
    hd = _dot(xn, win_ref[...])
    a = (jax.nn.silu(hd[:, :D_FF]) * hd[:, D_FF:]).astype(BF16)
    f = _dot(a, wout_ref[...])
    o_ref[...] = h + _rms(f, gpost_ref[...])


def _ffn_kernel(h_ref, gpre_ref, gpost_ref, win_ref, wout_ref, o_ref):
    _ffn_body(h_ref[...], gpre_ref, gpost_ref, win_ref, wout_ref, o_ref)


def _oproj_ffn_kernel(h_ref, att_ref, wo_ref, gmix_ref, gpre_ref, gpost_ref, win_ref, wout_ref, o_ref):
    h = h_ref[...] + _rms(_dot(att_ref[...], wo_ref[...]), gmix_ref[...])
    _ffn_body(h, gpre_ref, gpost_ref, win_ref, wout_ref, o_ref)


def _ffn_layer(h, g_pre, g_post, w_in, w_out, att=None, w_o=None, g_mix=None):
    seq = h.shape[0]
    row_spec = pl.BlockSpec((FFN_ROWS, D_MODEL), lambda i: (i, 0))
    tail = (g_pre.reshape(1, D_MODEL), g_post.reshape(1, D_MODEL), w_in.astype(BF16), w_out.astype(BF16))
    if att is None:
        kern, args, specs = _ffn_kernel, (h,) + tail, [row_spec]
    else:
        head = (att, w_o.astype(BF16), g_mix.reshape(1, D_MODEL))
        kern, args = _oproj_ffn_kernel, (h,) + head + tail
        specs = [row_spec, row_spec, _const_spec(head[1].shape), _const_spec(head[2].shape)]
    specs = specs + [_const_spec(a.shape) for a in tail]
    return pl.pallas_call(
        kern,
        name="ffn",
        grid=(seq // FFN_ROWS,),
        in_specs=specs,
        out_specs=row_spec,
        out_shape=jax.ShapeDtypeStruct((seq, D_MODEL), F32),
        compiler_params=pltpu.CompilerParams(
            dimension_semantics=("arbitrary",), vmem_limit_bytes=_vmem_limit(56 * 1024 * 1024)),
    )(*args)


def _q_of(h, gq_ref, wq_ref):
    u = _rms(h, gq_ref[...]).astype(BF16)
    return (_dot(u, wq_ref[...]) * (HEAD_DIM ** -0.5 * LOG2_E)).astype(BF16)


def _qkv_kernel(h_ref, gq_ref, wq_ref, gkv_ref, wkt_ref, wv_ref, q_ref, kt_ref, v_ref):
    h = h_ref[...]
    q_ref[...] = _q_of(h, gq_ref, wq_ref)
    hk = _rms(h, gkv_ref[...]).astype(BF16)
    kt_ref[...] = lax.dot_general(wkt_ref[...], hk, (((1,), (1,)), ((), ())),
                                  preferred_element_type=F32).astype(BF16)
    v_ref[...] = _dot(hk, wv_ref[...]).astype(BF16)


def _q_kernel(h_ref, gq_ref, wq_ref, q_ref):
    q_ref[...] = _q_of(h_ref[...], gq_ref, wq_ref)


def _qkv_proj(h, g_q, w_q, g_kv, w_k, w_v):
    seq = h.shape[0]
    row_spec = pl.BlockSpec((PROJ_ROWS, D_MODEL), lambda i: (i, 0))
    col_spec = pl.BlockSpec((D_MODEL, PROJ_ROWS), lambda i: (0, i))
    args = (h, g_q.reshape(1, D_MODEL), w_q.astype(BF16), g_kv.reshape(1, D_MODEL),
            w_k.T.astype(BF16), w_v.astype(BF16))
    return pl.pallas_call(
        _qkv_kernel,
        name="qkv_proj",
        grid=(seq // PROJ_ROWS,),
        in_specs=[row_spec] + [_const_spec(a.shape) for a in args[1:]],
        out_specs=[row_spec, col_spec, row_spec],
        out_shape=[jax.ShapeDtypeStruct((seq, D_MODEL), BF16),
                   jax.ShapeDtypeStruct((D_MODEL, seq), BF16),
                   jax.ShapeDtypeStruct((seq, D_MODEL), BF16)],
        compiler_params=pltpu.CompilerParams(
            dimension_semantics=("arbitrary",), vmem_limit_bytes=_vmem_limit(40 * 1024 * 1024)),
    )(*args)


def _q_proj(h, g_q, w_q):
    seq = h.shape[0]
    row_spec = pl.BlockSpec((PROJ_ROWS, D_MODEL), lambda i: (i, 0))
    args = (h, g_q.reshape(1, D_MODEL), w_q.astype(BF16))
    return pl.pallas_call(
        _q_kernel,
        name="q_proj",
        grid=(seq // PROJ_ROWS,),
        in_specs=[row_spec] + [_const_spec(a.shape) for a in args[1:]],
        out_specs=row_spec,
        out_shape=jax.ShapeDtypeStruct((seq, D_MODEL), BF16),
        compiler_params=pltpu.CompilerParams(
            dimension_semantics=("arbitrary",), vmem_limit_bytes=_vmem_limit(40 * 1024 * 1024)),
    )(*args)


def _attn_kernel(q_ref, kt_ref, v_ref, o_ref, qm_ref, acc_ref, lr_ref):
    tile = pl.program_id(1)
    lane = lax.broadcasted_iota(jnp.int32, (ATT_KEYS, V7X_LANES), 1)
    head0 = lane < HEAD_DIM
    srow = lax.broadcasted_iota(jnp.int32, (ATT_PAIR_ROWS, ATT_KEYS), 0)
    skey = lax.broadcasted_iota(jnp.int32, (ATT_PAIR_ROWS, ATT_KEYS), 1)
    causal = jnp.where(srow >= ATT_KEYS, srow - ATT_KEYS, srow) > skey
    trow = lax.broadcasted_iota(jnp.int32, (2 * ATT_KEYS, 2 * ATT_KEYS), 0)
    tcol = lax.broadcasted_iota(jnp.int32, (2 * ATT_KEYS, 2 * ATT_KEYS), 1)
    tkey = jnp.where(trow >= ATT_KEYS, trow - ATT_KEYS, trow)
    tri_ones = jnp.where((tkey >= tcol) | (tcol >= ATT_KEYS), 1.0, 0.0).astype(BF16)

    for qb in range(ATT_QBLOCKS):
        q = q_ref[qb * ATT_KEYS:(qb + 1) * ATT_KEYS, :]
        zero = jnp.zeros_like(q)
        qm_ref[qb] = jnp.concatenate([jnp.where(head0, q, zero), jnp.where(head0, zero, q)], axis=0)

    half = ATT_KEYS // 2
    early = [slice(hh * ATT_KEYS, hh * ATT_KEYS + half) for hh in range(HEADS_PER_STEP)]
    late = [slice(hh * ATT_KEYS + half, (hh + 1) * ATT_KEYS) for hh in range(HEADS_PER_STEP)]

    def pick(ref, qb, parts):
        return ref[qb] if parts is None else jnp.concatenate([ref[qb, p, :] for p in parts], axis=0)

    def step(kv_blocks, keep, state, parts=None):
        nrows = ATT_PAIR_ROWS if parts is None else ATT_PAIR_ROWS // 2
        starts = [pl.multiple_of(kb * ATT_KEYS, ATT_KEYS) for kb in kv_blocks]
        zs = [_dot(pick(qm_ref, qb, parts), kt_ref[:, pl.ds(starts[qb], ATT_KEYS)]) for qb in range(ATT_QBLOCKS)]
        sps = []
        for qb in range(ATT_QBLOCKS):
            sp = jnp.maximum(zs[qb], 0.0) + jnp.log2(1.0 + jnp.exp2(_neg_abs(zs[qb])))
            sps.append(sp if keep is None else jnp.where(keep[qb], sp, 0.0))
        sp = jnp.concatenate(sps, axis=0)
        sp_hi = sp.astype(BF16)
        sp_lo = (sp - sp_hi.astype(F32)).astype(BF16)
        sums = _dot(jnp.concatenate([sp_hi, sp_lo], axis=1), tri_ones)
        out = []
        for qb in range(ATT_QBLOCKS):
            rows = slice(qb * nrows, (qb + 1) * nrows)
            logw = zs[qb] - sums[rows, :ATT_KEYS]
            total = sums[rows, ATT_KEYS:]
            if state is not None:
                acc, log_rest = state[qb]
                logw = logw + log_rest
            w = jnp.exp2(logw)
            if keep is not None:
                w = jnp.where(keep[qb], w, 0.0)
            pv = _dot(w.astype(BF16), v_ref[pl.ds(starts[qb], ATT_KEYS), :])
            out.append((pv, -total) if state is None else (acc + pv, log_rest - total))
        return out

    def load_state(parts=None):
        return [(pick(acc_ref, qb, parts), pick(lr_ref, qb, parts)) for qb in range(ATT_QBLOCKS)]

    def store_state(state):
        top = None
        for qb, (acc, log_rest) in enumerate(state):
            acc_ref[qb] = acc
            lr_ref[qb] = log_rest
            top = log_rest if top is None else jnp.maximum(top, log_rest)
        top_early = jnp.max(jnp.maximum(top[early[0]], top[early[1]]))
        top_late = jnp.max(jnp.maximum(top[late[0]], top[late[1]]))
        return top_early, top_late

    def store_early(state):
        top = None
        for qb, (acc, log_rest) in enumerate(state):
            for n, p in enumerate(early):
                acc_ref[qb, p, :] = acc[n * half:(n + 1) * half]
                lr_ref[qb, p, :] = log_rest[n * half:(n + 1) * half]
            top = log_rest if top is None else jnp.maximum(top, log_rest)
        return jnp.max(top)

    first_block = tile * ATT_QBLOCKS
    tops = store_state(step([first_block + qb for qb in range(ATT_QBLOCKS)], [causal] * ATT_QBLOCKS, None))

    def walk_full(carry):
        s = carry[0]
        return (s + 1,) + store_state(step([first_block + qb - s for qb in range(ATT_QBLOCKS)], None, load_state()))

    def walk_early(carry):
        s, _, top_late = carry
        state = step([first_block + qb - s for qb in range(ATT_QBLOCKS)], None, load_state(early), early)
        return s + 1, store_early(state), top_late

    def walk_ragged(carry):
        s = carry[0]
        blocks = [first_block + qb - s for qb in range(ATT_QBLOCKS)]
        state = step([jnp.maximum(kb, 0) for kb in blocks], [kb >= 0 for kb in blocks], load_state())
        return (s + 1,) + store_state(state)

    carry = (jnp.int32(1),) + tops
    carry = lax.while_loop(lambda c: (c[0] <= first_block) & (c[2] > ATT_LOG2_CUTOFF), walk_full, carry)
    carry = lax.while_loop(lambda c: (c[0] <= first_block) & (c[1] > ATT_LOG2_CUTOFF), walk_early, carry)
    lax.while_loop(lambda c: (c[0] < first_block + ATT_QBLOCKS) & (jnp.maximum(c[1], c[2]) > ATT_LOG2_CUTOFF),
                   walk_ragged, carry)

    for qb in range(ATT_QBLOCKS):
        acc = acc_ref[qb]
        out = jnp.where(head0, acc[:ATT_KEYS], acc[ATT_KEYS:])
        o_ref[qb * ATT_KEYS:(qb + 1) * ATT_KEYS, :] = out.astype(BF16)


def _attention(q, kt, v):
    seq = q.shape[0]
    assert HEADS_PER_STEP == 2
    state = pltpu.VMEM((ATT_QBLOCKS, ATT_PAIR_ROWS, ATT_KEYS), F32)
    stacked_q = pltpu.VMEM((ATT_QBLOCKS, ATT_PAIR_ROWS, V7X_LANES), BF16)
    return pl.pallas_call(
        _attn_kernel,
        name="stickbreaking_attention",
        grid=(N_HEADS // HEADS_PER_STEP, seq // ATT_ROWS),
        in_specs=[pl.BlockSpec((ATT_ROWS, V7X_LANES), lambda p, i: (i, p)),
                  pl.BlockSpec((V7X_LANES, seq), lambda p, i: (p, 0)),
                  pl.BlockSpec((seq, V7X_LANES), lambda p, i: (0, p))],
        out_specs=pl.BlockSpec((ATT_ROWS, V7X_LANES), lambda p, i: (i, p)),
        out_shape=jax.ShapeDtypeStruct((seq, D_MODEL), BF16),
        scratch_shapes=[stacked_q, state, state],
        compiler_params=pltpu.CompilerParams(
            dimension_semantics=("arbitrary", "arbitrary"), vmem_limit_bytes=_vmem_limit(40 * 1024 * 1024)),
    )(q, kt, v)


def kernel(x, norm_mix_pre, norm_mix_post, norm_ffn_pre, norm_ffn_post, w_ffn_in, w_ffn_out, s5_log_dt, s5_a_re, s5_a_im, s5_b_re, s5_b_im, s5_c_re, s5_c_im, s5_d, s5_w_glu, s5_b_glu, kv_norm, w_k, w_v, w_q, w_o):
    bsz, seq, _ = x.shape
    assert bsz == 1 and seq % S5_ROWS == 0 and seq % ATT_ROWS == 0 and seq % FFN_ROWS == 0
    blocks = seq // S5_ROWS
    h = x.reshape(blocks, V7X_SUBLANES, S5_SEG, D_MODEL).transpose(0, 2, 1, 3).reshape(seq, D_MODEL)
    for layer in range(N_A_LAYERS):
        params = _s5_params(s5_log_dt[layer], s5_a_re[layer], s5_a_im[layer], s5_b_re[layer],
                            s5_b_im[layer], s5_c_re[layer], s5_c_im[layer])
        h = _s5_layer(h, norm_mix_pre[layer], norm_mix_post[layer], params, s5_d[layer],
                      s5_w_glu[layer], s5_b_glu[layer])
        h = _ffn_layer(h, norm_ffn_pre[layer], norm_ffn_post[layer], w_ffn_in[layer], w_ffn_out[layer])
    h = h.reshape(blocks, S5_SEG, V7X_SUBLANES, D_MODEL).transpose(0, 2, 1, 3).reshape(seq, D_MODEL)

    kt = v = None
    for layer in range(N_A_LAYERS, DEPTH):
        b = layer - N_A_LAYERS
        if b == 0:
            q, kt, v = _qkv_proj(h, norm_mix_pre[layer], w_q[b], kv_norm, w_k, w_v)
        else:
            q = _q_proj(h, norm_mix_pre[layer], w_q[b])
        att = _attention(q, kt, v)
        h = _ffn_layer(h, norm_ffn_pre[layer], norm_ffn_post[layer], w_ffn_in[layer], w_ffn_out[layer],
                       att=att, w_o=w_o[b], g_mix=norm_mix_post[layer])
    return h.reshape(bsz, seq, D_MODEL)
```

```python
import functools
import math

import jax
import jax.numpy as jnp
from jax import lax
from jax.experimental import pallas as pl
from jax.experimental.pallas import tpu as pltpu

F32 = jnp.float32
BF16 = jnp.bfloat16

D_MODEL = 1024
DEPTH = 4
N_A_LAYERS = DEPTH // 2
SSM_GROUP = 16
SSM_GROUPS = D_MODEL // SSM_GROUP
SSM_STATE = 64
N_HEADS = 16
HEAD_DIM = D_MODEL // N_HEADS
D_FF = 2816
EPS = 1e-6

V7X_LANES = 128
V7X_SUBLANES = 8
BF16_ROWS = 2 * V7X_SUBLANES
V7X_MXU_DIM = 256
V7X_VMEM_BYTES = 64 * 1024 * 1024

S5_KTILES = D_MODEL // V7X_MXU_DIM
S5_TILE_GROUPS = V7X_MXU_DIM // SSM_GROUP
S5_TILE_STATES = S5_TILE_GROUPS * SSM_STATE
S5_SEG = 64
S5_ROWS = V7X_SUBLANES * S5_SEG
S5_COLS = 512

FFN_ROWS = 512
PROJ_ROWS = 512

ATT_KEYS = 128
ATT_QBLOCKS = 8
ATT_ROWS = ATT_QBLOCKS * ATT_KEYS
HEADS_PER_STEP = V7X_LANES // HEAD_DIM
ATT_PAIR_ROWS = HEADS_PER_STEP * ATT_KEYS
LOG2_E = math.log2(math.e)
ATT_LOG2_CUTOFF = -104.0 * LOG2_E


def _vmem_limit(nbytes):
    return int(min(nbytes, V7X_VMEM_BYTES - 6 * 1024 * 1024))


def _rms(x, g):
    return x * lax.rsqrt(jnp.mean(x * x, axis=-1, keepdims=True) + EPS) * g


def _dot(a, b):
    return jnp.dot(a, b, preferred_element_type=F32)


def _neg_abs(x):
    return -jnp.abs(x)


def _const_spec(shape):
    zeros = (0,) * len(shape)
    return pl.BlockSpec(shape, lambda *_: zeros, pipeline_mode=pl.Buffered(1))


def _s5_kernel(h_ref, gpre_ref, gpost_ref, wb_ref, wc_ref, lre_ref, lim_ref, mre_ref, mim_ref,
               d_ref, wglu_ref, bglu_ref, o_ref,
               xre_ref, xim_ref, xb_ref, cre_ref, cim_ref, y_ref):
    @pl.when(pl.program_id(0) == 0)
    def _():
        cre_ref[...] = jnp.zeros_like(cre_ref)
        cim_ref[...] = jnp.zeros_like(cim_ref)

    h = h_ref[...]
    u = _rms(h, gpre_ref[...])
    u_bf = u.astype(BF16)
    zeros = jnp.zeros((V7X_SUBLANES, S5_COLS), F32)
    row = lax.broadcasted_iota(jnp.int32, (V7X_SUBLANES, S5_COLS), 0)

    for kt in range(S5_KTILES):
        slot = kt % 2
        bu = _dot(u_bf[:, kt * V7X_MXU_DIM:(kt + 1) * V7X_MXU_DIM], wb_ref[kt])
        xre_ref[slot] = bu[:, :S5_TILE_STATES]
        xim_ref[slot] = bu[:, S5_TILE_STATES:]

        for cg in range(S5_TILE_STATES // S5_COLS):
            cs = slice(cg * S5_COLS, (cg + 1) * S5_COLS)
            lre = jnp.broadcast_to(lre_ref[kt, :, cs], (V7X_SUBLANES, S5_COLS))
            lim = jnp.broadcast_to(lim_ref[kt, :, cs], (V7X_SUBLANES, S5_COLS))

            er, ei = zeros, zeros
            for i in range(S5_SEG):
                rows = pl.ds(i * V7X_SUBLANES, V7X_SUBLANES)
                er, ei = (lre * er - lim * ei + xre_ref[slot, rows, cs],
                          lre * ei + lim * er + xim_ref[slot, rows, cs])
                xre_ref[slot, rows, cs] = er
                xim_ref[slot, rows, cs] = ei

            mre = mre_ref[kt, :, cs]
            mim = mim_ref[kt, :, cs]
            cur_r = cre_ref[kt, :, cs]
            cur_i = cim_ref[kt, :, cs]
            xin_r = zeros
            xin_i = zeros
            for r in range(V7X_SUBLANES):
                xin_r = jnp.where(row == r, cur_r, xin_r)
                xin_i = jnp.where(row == r, cur_i, xin_i)
                nxt_r = mre * cur_r - mim * cur_i + er[r:r + 1, :]
                nxt_i = mre * cur_i + mim * cur_r + ei[r:r + 1, :]
                cur_r, cur_i = nxt_r, nxt_i
            cre_ref[kt, :, cs] = cur_r
            cim_ref[kt, :, cs] = cur_i

            cs_im = slice(S5_TILE_STATES + cg * S5_COLS, S5_TILE_STATES + (cg + 1) * S5_COLS)

            cr, ci = xin_r, xin_i
            for j in range(S5_ROWS // BF16_ROWS):
                full_r, full_i = [], []
                for half in range(BF16_ROWS // V7X_SUBLANES):
                    cr, ci = lre * cr - lim * ci, lre * ci + lim * cr
                    rows = pl.ds(j * BF16_ROWS + half * V7X_SUBLANES, V7X_SUBLANES)
                    full_r.append(xre_ref[slot, rows, cs] + cr)
                    full_i.append(xim_ref[slot, rows, cs] + ci)
                packed = pl.ds(j * BF16_ROWS, BF16_ROWS)
                xb_ref[slot, packed, cs] = jnp.concatenate(full_r, axis=0).astype(BF16)
                xb_ref[slot, packed, cs_im] = jnp.concatenate(full_i, axis=0).astype(BF16)

        y_ref[:, kt * V7X_MXU_DIM:(kt + 1) * V7X_MXU_DIM] = _dot(xb_ref[slot], wc_ref[kt])

    y = y_ref[...] + d_ref[...] * u
    a = jax.nn.gelu(y).astype(BF16)
    z = _dot(a, wglu_ref[...]) + bglu_ref[...]
    mix = z[:, :D_MODEL] * jax.nn.sigmoid(z[:, D_MODEL:])
    o_ref[...] = h + _rms(mix, gpost_ref[...])


def _s5_params(log_dt, a_re, a_im, b_re, b_im, c_re, c_im):
    dt = jnp.exp(log_dt.astype(F32))[:, None]
    a_re = a_re.astype(F32)
    a_im = a_im.astype(F32)
    mag = jnp.exp(a_re * dt)
    lam_re = mag * jnp.cos(a_im * dt)
    lam_im = mag * jnp.sin(a_im * dt)
    den = a_re * a_re + a_im * a_im
    nr = lam_re - 1.0
    coef_re = (nr * a_re + lam_im * a_im) / den
    coef_im = (lam_im * a_re - nr * a_im) / den
    bb_re = coef_re[..., None] * b_re - coef_im[..., None] * b_im
    bb_im = coef_re[..., None] * b_im + coef_im[..., None] * b_re

    eye = jnp.eye(S5_TILE_GROUPS, dtype=F32)

    def in_proj(b):
        b = b.reshape(S5_KTILES, S5_TILE_GROUPS, SSM_STATE, SSM_GROUP)
        w = jnp.einsum('kgpc,gh->kgchp', b, eye)
        return w.reshape(S5_KTILES, V7X_MXU_DIM, S5_TILE_STATES)

    def out_proj(c):
        c = c.astype(F32).reshape(S5_KTILES, S5_TILE_GROUPS, SSM_GROUP, SSM_STATE)
        w = jnp.einsum('kgcp,gh->kgphc', c, eye)
        return w.reshape(S5_KTILES, S5_TILE_STATES, V7X_MXU_DIM)

    wb = jnp.concatenate([in_proj(bb_re), in_proj(bb_im)], axis=2).astype(BF16)
    wc = jnp.concatenate([out_proj(c_re), -out_proj(c_im)], axis=1).astype(BF16)

    def tile(v):
        return v.reshape(S5_KTILES, 1, S5_TILE_STATES)

    seg_mag = jnp.exp(float(S5_SEG) * a_re * dt)
    m_re = seg_mag * jnp.cos(float(S5_SEG) * a_im * dt)
    m_im = seg_mag * jnp.sin(float(S5_SEG) * a_im * dt)
    return wb, wc, tile(lam_re), tile(lam_im), tile(m_re), tile(m_im)


def _s5_layer(h, g_pre, g_post, params, d_skip, w_glu, b_glu):
    seq = h.shape[0]
    row_spec = pl.BlockSpec((S5_ROWS, D_MODEL), lambda i: (i, 0))
    args = (h, g_pre.reshape(1, D_MODEL), g_post.reshape(1, D_MODEL),
            *params, d_skip.reshape(1, D_MODEL).astype(F32), w_glu.astype(BF16),
            b_glu.reshape(1, 2 * D_MODEL).astype(F32))
    in_specs = [row_spec] + [_const_spec(a.shape) for a in args[1:]]
    return pl.pallas_call(
        _s5_kernel,
        name="s5_mixer",
        grid=(seq // S5_ROWS,),
        in_specs=in_specs,
        out_specs=row_spec,
        out_shape=jax.ShapeDtypeStruct((seq, D_MODEL), F32),
        scratch_shapes=[
            pltpu.VMEM((2, S5_ROWS, S5_TILE_STATES), F32),
            pltpu.VMEM((2, S5_ROWS, S5_TILE_STATES), F32),
            pltpu.VMEM((2, S5_ROWS, 2 * S5_TILE_STATES), BF16),
            pltpu.VMEM((S5_KTILES, 1, S5_TILE_STATES), F32),
            pltpu.VMEM((S5_KTILES, 1, S5_TILE_STATES), F32),
            pltpu.VMEM((S5_ROWS, D_MODEL), F32),
        ],
        compiler_params=pltpu.CompilerParams(
            dimension_semantics=("arbitrary",), vmem_limit_bytes=_vmem_limit(56 * 1024 * 1024)),
    )(*args)


def _ffn_body(h, gpre_ref, gpost_ref, win_ref, wout_ref, o_ref):
    xn = _rms(h, gpre_ref[...]).astype(BF16)
    hd = _dot(xn, win_ref[...])
    a = (jax.nn.silu(hd[:, :D_FF]) * hd[:, D_FF:]).astype(BF16)
    f = _dot(a, wout_ref[...])
    o_ref[...] = h + _rms(f, gpost_ref[...])


def _ffn_kernel(h_ref, gpre_ref, gpost_ref, win_ref, wout_ref, o_ref):
    _ffn_body(h_ref[...], gpre_ref, gpost_ref, win_ref, wout_ref, o_ref)


def _oproj_ffn_kernel(h_ref, att_ref, wo_ref, gmix_ref, gpre_ref, gpost_ref, win_ref, wout_ref, o_ref):
    h = h_ref[...] + _rms(_dot(att_ref[...], wo_ref[...]), gmix_ref[...])
    _ffn_body(h, gpre_ref, gpost_ref, win_ref, wout_ref, o_ref)


def _ffn_layer(h, g_pre, g_post, w_in, w_out, att=None, w_o=None, g_mix=None):
    seq = h.shape[0]
    row_spec = pl.BlockSpec((FFN_ROWS, D_MODEL), lambda i: (i, 0))
    tail = (g_pre.reshape(1, D_MODEL), g_post.reshape(1, D_MODEL), w_in.astype(BF16), w_out.astype(BF16))
    if att is None:
        kern, args, specs = _ffn_kernel, (h,) + tail, [row_spec]
    else:
        head = (att, w_o.astype(BF16), g_mix.reshape(1, D_MODEL))
        kern, args = _oproj_ffn_kernel, (h,) + head + tail
        specs = [row_spec, row_spec, _const_spec(head[1].shape), _const_spec(head[2].shape)]
    specs = specs + [_const_spec(a.shape) for a in tail]
    return pl.pallas_call(
        kern,
        name="ffn",
        grid=(seq // FFN_ROWS,),
        in_specs=specs,
        out_specs=row_spec,
        out_shape=jax.ShapeDtypeStruct((seq, D_MODEL), F32),
        compiler_params=pltpu.CompilerParams(
            dimension_semantics=("arbitrary",), vmem_limit_bytes=_vmem_limit(56 * 1024 * 1024)),
    )(*args)


def _q_of(h, gq_ref, wq_ref):
    u = _rms(h, gq_ref[...]).astype(BF16)
    return (_dot(u, wq_ref[...]) * (HEAD_DIM ** -0.5 * LOG2_E)).astype(BF16)


def _qkv_kernel(h_ref, gq_ref, wq_ref, gkv_ref, wkt_ref, wv_ref, q_ref, kt_ref, v_ref):
    h = h_ref[...]
    q_ref[...] = _q_of(h, gq_ref, wq_ref)
    hk = _rms(h, gkv_ref[...]).astype(BF16)
    kt_ref[...] = lax.dot_general(wkt_ref[...], hk, (((1,), (1,)), ((), ())),
                                  preferred_element_type=F32).astype(BF16)
    v_ref[...] = _dot(hk, wv_ref[...]).astype(BF16)


def _q_kernel(h_ref, gq_ref, wq_ref, q_ref):
    q_ref[...] = _q_of(h_ref[...], gq_ref, wq_ref)


def _qkv_proj(h, g_q, w_q, g_kv, w_k, w_v):
    seq = h.shape[0]
    row_spec = pl.BlockSpec((PROJ_ROWS, D_MODEL), lambda i: (i, 0))
    col_spec = pl.BlockSpec((D_MODEL, PROJ_ROWS), lambda i: (0, i))
    args = (h, g_q.reshape(1, D_MODEL), w_q.astype(BF16), g_kv.reshape(1, D_MODEL),
            w_k.T.astype(BF16), w_v.astype(BF16))
    return pl.pallas_call(
        _qkv_kernel,
        name="qkv_proj",
        grid=(seq // PROJ_ROWS,),
        in_specs=[row_spec] + [_const_spec(a.shape) for a in args[1:]],
        out_specs=[row_spec, col_spec, row_spec],
        out_shape=[jax.ShapeDtypeStruct((seq, D_MODEL), BF16),
                   jax.ShapeDtypeStruct((D_MODEL, seq), BF16),
                   jax.ShapeDtypeStruct((seq, D_MODEL), BF16)],
        compiler_params=pltpu.CompilerParams(
            dimension_semantics=("arbitrary",), vmem_limit_bytes=_vmem_limit(40 * 1024 * 1024)),
    )(*args)


def _q_proj(h, g_q, w_q):
    seq = h.shape[0]
    row_spec = pl.BlockSpec((PROJ_ROWS, D_MODEL), lambda i: (i, 0))
    args = (h, g_q.reshape(1, D_MODEL), w_q.astype(BF16))
    return pl.pallas_call(
        _q_kernel,
        name="q_proj",
        grid=(seq // PROJ_ROWS,),
        in_specs=[row_spec] + [_const_spec(a.shape) for a in args[1:]],
        out_specs=row_spec,
        out_shape=jax.ShapeDtypeStruct((seq, D_MODEL), BF16),
        compiler_params=pltpu.CompilerParams(
            dimension_semantics=("arbitrary",), vmem_limit_bytes=_vmem_limit(40 * 1024 * 1024)),
    )(*args)


def _attn_kernel(q_ref, kt_ref, v_ref, o_ref, qm_ref, acc_ref, lr_ref):
    tile = pl.program_id(1)
    lane = lax.broadcasted_iota(jnp.int32, (ATT_KEYS, V7X_LANES), 1)
    head0 = lane < HEAD_DIM
    srow = lax.broadcasted_iota(jnp.int32, (ATT_PAIR_ROWS, ATT_KEYS), 0)
    skey = lax.broadcasted_iota(jnp.int32, (ATT_PAIR_ROWS, ATT_KEYS), 1)
    causal = jnp.where(srow >= ATT_KEYS, srow - ATT_KEYS, srow) > skey
    trow = lax.broadcasted_iota(jnp.int32, (2 * ATT_KEYS, 2 * ATT_KEYS), 0)
    tcol = lax.broadcasted_iota(jnp.int32, (2 * ATT_KEYS, 2 * ATT_KEYS), 1)
    tkey = jnp.where(trow >= ATT_KEYS, trow - ATT_KEYS, trow)
    tri_ones = jnp.where((tkey >= tcol) | (tcol >= ATT_KEYS), 1.0, 0.0).astype(BF16)

    for qb in range(ATT_QBLOCKS):
        q = q_ref[qb * ATT_KEYS:(qb + 1) * ATT_KEYS, :]
        zero = jnp.zeros_like(q)
        qm_ref[qb] = jnp.concatenate([jnp.where(head0, q, zero), jnp.where(head0, zero, q)], axis=0)

    half = ATT_KEYS // 2
    early = [slice(hh * ATT_KEYS, hh * ATT_KEYS + half) for hh in range(HEADS_PER_STEP)]
    late = [slice(hh * ATT_KEYS + half, (hh + 1) * ATT_KEYS) for hh in range(HEADS_PER_STEP)]

    def pick(ref, qb, parts):
        return ref[qb] if parts is None else jnp.concatenate([ref[qb, p, :] for p in parts], axis=0)

    def step(kv_blocks, keep, state, parts=None):
        nrows = ATT_PAIR_ROWS if parts is None else ATT_PAIR_ROWS // 2
        starts = [pl.multiple_of(kb * ATT_KEYS, ATT_KEYS) for kb in kv_blocks]
        zs = [_dot(pick(qm_ref, qb, parts), kt_ref[:, pl.ds(starts[qb], ATT_KEYS)]) for qb in range(ATT_QBLOCKS)]
        sps = []
        for qb in range(ATT_QBLOCKS):
            sp = jnp.maximum(zs[qb], 0.0) + jnp.log2(1.0 + jnp.exp2(_neg_abs(zs[qb])))
            sps.append(sp if keep is None else jnp.where(keep[qb], sp, 0.0))
        sp = jnp.concatenate(sps, axis=0)
        sp_hi = sp.astype(BF16)
        sp_lo = (sp - sp_hi.astype(F32)).astype(BF16)
        sums = _dot(jnp.concatenate([sp_hi, sp_lo], axis=1), tri_ones)
        out = []
        for qb in range(ATT_QBLOCKS):
            rows = slice(qb * nrows, (qb + 1) * nrows)
            logw = zs[qb] - sums[rows, :ATT_KEYS]
            total = sums[rows, ATT_KEYS:]
            if state is not None:
                acc, log_rest = state[qb]
                logw = logw + log_rest
            w = jnp.exp2(logw)
            if keep is not None:
                w = jnp.where(keep[qb], w, 0.0)
            pv = _dot(w.astype(BF16), v_ref[pl.ds(starts[qb], ATT_KEYS), :])
            out.append((pv, -total) if state is None else (acc + pv, log_rest - total))
        return out

    def load_state(parts=None):
        return [(pick(acc_ref, qb, parts), pick(lr_ref, qb, parts)) for qb in range(ATT_QBLOCKS)]

    def store_state(state):
        top = None
        for qb, (acc, log_rest) in enumerate(state):
            acc_ref[qb] = acc
            lr_ref[qb] = log_rest
            top = log_rest if top is None else jnp.maximum(top, log_rest)
        top_early = jnp.max(jnp.maximum(top[early[0]], top[early[1]]))
        top_late = jnp.max(jnp.maximum(top[late[0]], top[late[1]]))
        return top_early, top_late

    def store_early(state):
        top = None
        for qb, (acc, log_rest) in enumerate(state):
            for n, p in enumerate(early):
                acc_ref[qb, p, :] = acc[n * half:(n + 1) * half]
                lr_ref[qb, p, :] = log_rest[n * half:(n + 1) * half]
            top = log_rest if top is None else jnp.maximum(top, log_rest)
        return jnp.max(top)

    first_block = tile * ATT_QBLOCKS
    state = step([first_block + qb for qb in range(ATT_QBLOCKS)], [causal] * ATT_QBLOCKS, None)
    before = [first_block + qb - 1 for qb in range(ATT_QBLOCKS)]
    tops = store_state(step([jnp.maximum(kb, 0) for kb in before], [kb >= 0 for kb in before], state))

    def walk_full(carry):
        s = carry[0]
        return (s + 1,) + store_state(step([first_block + qb - s for qb in range(ATT_QBLOCKS)], None, load_state()))

    def walk_early(carry):
        s, _, top_late = carry
        state = step([first_block + qb - s for qb in range(ATT_QBLOCKS)], None, load_state(early), early)
        return s + 1, store_early(state), top_late

    def walk_ragged(carry):
        s = carry[0]
        blocks = [first_block + qb - s for qb in range(ATT_QBLOCKS)]
        state = step([jnp.maximum(kb, 0) for kb in blocks], [kb >= 0 for kb in blocks], load_state())
        return (s + 1,) + store_state(state)

    carry = (jnp.int32(2),) + tops
    carry = lax.while_loop(lambda c: (c[0] <= first_block) & (c[2] > ATT_LOG2_CUTOFF), walk_full, carry)
    carry = lax.while_loop(lambda c: (c[0] <= first_block) & (c[1] > ATT_LOG2_CUTOFF), walk_early, carry)
    lax.while_loop(lambda c: (c[0] < first_block + ATT_QBLOCKS) & (jnp.maximum(c[1], c[2]) > ATT_LOG2_CUTOFF),
                   walk_ragged, carry)

    for qb in range(ATT_QBLOCKS):
        acc = acc_ref[qb]
        out = jnp.where(head0, acc[:ATT_KEYS], acc[ATT_KEYS:])
        o_ref[qb * ATT_KEYS:(qb + 1) * ATT_KEYS, :] = out.astype(BF16)


def _attention(q, kt, v):
    seq = q.shape[0]
    assert HEADS_PER_STEP == 2
    state = pltpu.VMEM((ATT_QBLOCKS, ATT_PAIR_ROWS, ATT_KEYS), F32)
    stacked_q = pltpu.VMEM((ATT_QBLOCKS, ATT_PAIR_ROWS, V7X_LANES), BF16)
    return pl.pallas_call(
        _attn_kernel,
        name="stickbreaking_attention",
        grid=(N_HEADS // HEADS_PER_STEP, seq // ATT_ROWS),
        in_specs=[pl.BlockSpec((ATT_ROWS, V7X_LANES), lambda p, i: (i, p)),
                  pl.BlockSpec((V7X_LANES, seq), lambda p, i: (p, 0)),
                  pl.BlockSpec((seq, V7X_LANES), lambda p, i: (0, p))],
        out_specs=pl.BlockSpec((ATT_ROWS, V7X_LANES), lambda p, i: (i, p)),
        out_shape=jax.ShapeDtypeStruct((seq, D_MODEL), BF16),
        scratch_shapes=[stacked_q, state, state],
        compiler_params=pltpu.CompilerParams(
            dimension_semantics=("arbitrary", "arbitrary"), vmem_limit_bytes=_vmem_limit(40 * 1024 * 1024)),
    )(q, kt, v)


def kernel(x, norm_mix_pre, norm_mix_post, norm_ffn_pre, norm_ffn_post, w_ffn_in, w_ffn_out, s5_log_dt, s5_a_re, s5_a_im, s5_b_re, s5_b_im, s5_c_re, s5_c_im, s5_d, s5_w_glu, s5_b_glu, kv_norm, w_k, w_v, w_q, w_o):
    bsz, seq, _ = x.shape
    assert bsz == 1 and seq % S5_ROWS == 0 and seq % ATT_ROWS == 0 and seq % FFN_ROWS == 0
    blocks = seq // S5_ROWS
    h = x.reshape(blocks, V7X_SUBLANES, S5_SEG, D_MODEL).transpose(0, 2, 1, 3).reshape(seq, D_MODEL)
    s5_params = jax.vmap(_s5_params)(s5_log_dt, s5_a_re, s5_a_im, s5_b_re, s5_b_im, s5_c_re, s5_c_im)
    for layer in range(N_A_LAYERS):
        params = tuple(p[layer] for p in s5_params)
        h = _s5_layer(h, norm_mix_pre[layer], norm_mix_post[layer], params, s5_d[layer],
                      s5_w_glu[layer], s5_b_glu[layer])
        h = _ffn_layer(h, norm_ffn_pre[layer], norm_ffn_post[layer], w_ffn_in[layer], w_ffn_out[layer])
    h = h.reshape(blocks, S5_SEG, V7X_SUBLANES, D_MODEL).transpose(0, 2, 1, 3).reshape(seq, D_MODEL)

    kt = v = None
    for layer in range(N_A_LAYERS, DEPTH):
        b = layer - N_A_LAYERS
        if b == 0:
            q, kt, v = _qkv_proj(h, norm_mix_pre[layer], w_q[b], kv_norm, w_k, w_v)
        else:
            q = _q_proj(h, norm_mix_pre[layer], w_q[b])
        att = _attention(q, kt, v)
        h = _ffn_layer(h, norm_ffn_pre[layer], norm_ffn_post[layer], w_ffn_in[layer], w_ffn_out[layer],
                       att=att, w_o=w_o[b], g_mix=norm_mix_post[layer])
    return h.reshape(bsz, seq, D_MODEL)
```

```python
import functools
import math

import jax
import jax.numpy as jnp
from jax import lax
from jax.experimental import pallas as pl
from jax.experimental.pallas import tpu as pltpu

F32 = jnp.float32
BF16 = jnp.bfloat16

D_MODEL = 1024
DEPTH = 4
N_A_LAYERS = DEPTH // 2
SSM_GROUP = 16
SSM_GROUPS = D_MODEL // SSM_GROUP
SSM_STATE = 64
N_HEADS = 16
HEAD_DIM = D_MODEL // N_HEADS
D_FF = 2816
EPS = 1e-6

V7X_LANES = 128
V7X_SUBLANES = 8
BF16_ROWS = 2 * V7X_SUBLANES
V7X_MXU_DIM = 256
V7X_VMEM_BYTES = 64 * 1024 * 1024

S5_KTILES = D_MODEL // V7X_MXU_DIM
S5_TILE_GROUPS = V7X_MXU_DIM // SSM_GROUP
S5_TILE_STATES = S5_TILE_GROUPS * SSM_STATE
S5_SEG = 64
S5_ROWS = V7X_SUBLANES * S5_SEG
S5_COLS = 512

FFN_ROWS = 512
PROJ_ROWS = 512

ATT_KEYS = 128
ATT_QBLOCKS = 16
ATT_ROWS = ATT_QBLOCKS * ATT_KEYS
HEADS_PER_STEP = V7X_LANES // HEAD_DIM
ATT_PAIR_ROWS = HEADS_PER_STEP * ATT_KEYS
LOG2_E = math.log2(math.e)
ATT_LOG2_CUTOFF = -104.0 * LOG2_E


def _vmem_limit(nbytes):
    return int(min(nbytes, V7X_VMEM_BYTES - 6 * 1024 * 1024))


def _rms(x, g):
    return x * lax.rsqrt(jnp.mean(x * x, axis=-1, keepdims=True) + EPS) * g


def _dot(a, b):
    return jnp.dot(a, b, preferred_element_type=F32)


def _neg_abs(x):
    return -jnp.abs(x)


def _const_spec(shape):
    zeros = (0,) * len(shape)
    return pl.BlockSpec(shape, lambda *_: zeros, pipeline_mode=pl.Buffered(1))


def _s5_kernel(h_ref, gpre_ref, gpost_ref, wb_ref, wc_ref, lre_ref, lim_ref, mre_ref, mim_ref,
               d_ref, wglu_ref, bglu_ref, o_ref,
               xre_ref, xim_ref, xb_ref, cre_ref, cim_ref, y_ref):
    @pl.when(pl.program_id(0) == 0)
    def _():
        cre_ref[...] = jnp.zeros_like(cre_ref)
        cim_ref[...] = jnp.zeros_like(cim_ref)

    h = h_ref[...]
    u = _rms(h, gpre_ref[...])
    u_bf = u.astype(BF16)
    zeros = jnp.zeros((V7X_SUBLANES, S5_COLS), F32)
    row = lax.broadcasted_iota(jnp.int32, (V7X_SUBLANES, S5_COLS), 0)

    for kt in range(S5_KTILES):
        slot = kt % 2
        bu = _dot(u_bf[:, kt * V7X_MXU_DIM:(kt + 1) * V7X_MXU_DIM], wb_ref[kt])
        xre_ref[slot] = bu[:, :S5_TILE_STATES]
        xim_ref[slot] = bu[:, S5_TILE_STATES:]

        for cg in range(S5_TILE_STATES // S5_COLS):
            cs = slice(cg * S5_COLS, (cg + 1) * S5_COLS)
            lre = jnp.broadcast_to(lre_ref[kt, :, cs], (V7X_SUBLANES, S5_COLS))
            lim = jnp.broadcast_to(lim_ref[kt, :, cs], (V7X_SUBLANES, S5_COLS))

            er, ei = zeros, zeros
            for i in range(S5_SEG):
                rows = pl.ds(i * V7X_SUBLANES, V7X_SUBLANES)
                er, ei = (lre * er - lim * ei + xre_ref[slot, rows, cs],
                          lre * ei + lim * er + xim_ref[slot, rows, cs])
                xre_ref[slot, rows, cs] = er
                xim_ref[slot, rows, cs] = ei

            mre = mre_ref[kt, :, cs]
            mim = mim_ref[kt, :, cs]
            cur_r = cre_ref[kt, :, cs]
            cur_i = cim_ref[kt, :, cs]
            xin_r = zeros
            xin_i = zeros
            for r in range(V7X_SUBLANES):
                xin_r = jnp.where(row == r, cur_r, xin_r)
                xin_i = jnp.where(row == r, cur_i, xin_i)
                nxt_r = mre * cur_r - mim * cur_i + er[r:r + 1, :]
                nxt_i = mre * cur_i + mim * cur_r + ei[r:r + 1, :]
                cur_r, cur_i = nxt_r, nxt_i
            cre_ref[kt, :, cs] = cur_r
            cim_ref[kt, :, cs] = cur_i

            cs_im = slice(S5_TILE_STATES + cg * S5_COLS, S5_TILE_STATES + (cg + 1) * S5_COLS)

            cr, ci = xin_r, xin_i
            for j in range(S5_ROWS // BF16_ROWS):
                full_r, full_i = [], []
                for half in range(BF16_ROWS // V7X_SUBLANES):
                    cr, ci = lre * cr - lim * ci, lre * ci + lim * cr
                    rows = pl.ds(j * BF16_ROWS + half * V7X_SUBLANES, V7X_SUBLANES)
                    full_r.append(xre_ref[slot, rows, cs] + cr)
                    full_i.append(xim_ref[slot, rows, cs] + ci)
                packed = pl.ds(j * BF16_ROWS, BF16_ROWS)
                xb_ref[slot, packed, cs] = jnp.concatenate(full_r, axis=0).astype(BF16)
                xb_ref[slot, packed, cs_im] = jnp.concatenate(full_i, axis=0).astype(BF16)

        y_ref[:, kt * V7X_MXU_DIM:(kt + 1) * V7X_MXU_DIM] = _dot(xb_ref[slot], wc_ref[kt])

    y = y_ref[...] + d_ref[...] * u
    a = jax.nn.gelu(y).astype(BF16)
    z = _dot(a, wglu_ref[...]) + bglu_ref[...]
    mix = z[:, :D_MODEL] * jax.nn.sigmoid(z[:, D_MODEL:])
    o_ref[...] = h + _rms(mix, gpost_ref[...])


def _s5_params(log_dt, a_re, a_im, b_re, b_im, c_re, c_im):
    dt = jnp.exp(log_dt.astype(F32))[:, None]
    a_re = a_re.astype(F32)
    a_im = a_im.astype(F32)
    mag = jnp.exp(a_re * dt)
    lam_re = mag * jnp.cos(a_im * dt)
    lam_im = mag * jnp.sin(a_im * dt)
    den = a_re * a_re + a_im * a_im
    nr = lam_re - 1.0
    coef_re = (nr * a_re + lam_im * a_im) / den
    coef_im = (lam_im * a_re - nr * a_im) / den
    bb_re = coef_re[..., None] * b_re - coef_im[..., None] * b_im
    bb_im = coef_re[..., None] * b_im + coef_im[..., None] * b_re

    eye = jnp.eye(S5_TILE_GROUPS, dtype=F32)

    def in_proj(b):
        b = b.reshape(S5_KTILES, S5_TILE_GROUPS, SSM_STATE, SSM_GROUP)
        w = jnp.einsum('kgpc,gh->kgchp', b, eye)
        return w.reshape(S5_KTILES, V7X_MXU_DIM, S5_TILE_STATES)

    def out_proj(c):
        c = c.astype(F32).reshape(S5_KTILES, S5_TILE_GROUPS, SSM_GROUP, SSM_STATE)
        w = jnp.einsum('kgcp,gh->kgphc', c, eye)
        return w.reshape(S5_KTILES, S5_TILE_STATES, V7X_MXU_DIM)

    wb = jnp.concatenate([in_proj(bb_re), in_proj(bb_im)], axis=2).astype(BF16)
    wc = jnp.concatenate([out_proj(c_re), -out_proj(c_im)], axis=1).astype(BF16)

    def tile(v):
        return v.reshape(S5_KTILES, 1, S5_TILE_STATES)

    seg_mag = jnp.exp(float(S5_SEG) * a_re * dt)
    m_re = seg_mag * jnp.cos(float(S5_SEG) * a_im * dt)
    m_im = seg_mag * jnp.sin(float(S5_SEG) * a_im * dt)
    return wb, wc, tile(lam_re), tile(lam_im), tile(m_re), tile(m_im)


def _s5_layer(h, g_pre, g_post, params, d_skip, w_glu, b_glu):
    seq = h.shape[0]
    row_spec = pl.BlockSpec((S5_ROWS, D_MODEL), lambda i: (i, 0))
    args = (h, g_pre.reshape(1, D_MODEL), g_post.reshape(1, D_MODEL),
            *params, d_skip.reshape(1, D_MODEL).astype(F32), w_glu.astype(BF16),
            b_glu.reshape(1, 2 * D_MODEL).astype(F32))
    in_specs = [row_spec] + [_const_spec(a.shape) for a in args[1:]]
    return pl.pallas_call(
        _s5_kernel,
        name="s5_mixer",
        grid=(seq // S5_ROWS,),
        in_specs=in_specs,
        out_specs=row_spec,
        out_shape=jax.ShapeDtypeStruct((seq, D_MODEL), F32),
        scratch_shapes=[
            pltpu.VMEM((2, S5_ROWS, S5_TILE_STATES), F32),
            pltpu.VMEM((2, S5_ROWS, S5_TILE_STATES), F32),
            pltpu.VMEM((2, S5_ROWS, 2 * S5_TILE_STATES), BF16),
            pltpu.VMEM((S5_KTILES, 1, S5_TILE_STATES), F32),
            pltpu.VMEM((S5_KTILES, 1, S5_TILE_STATES), F32),
            pltpu.VMEM((S5_ROWS, D_MODEL), F32),
        ],
        compiler_params=pltpu.CompilerParams(
            dimension_semantics=("arbitrary",), vmem_limit_bytes=_vmem_limit(56 * 1024 * 1024)),
    )(*args)


def _ffn_body(h, gpre_ref, gpost_ref, win_ref, wout_ref, o_ref):
    xn = _rms(h, gpre_ref[...]).astype(BF16)
    hd = _dot(xn, win_ref[...])
    a = (jax.nn.silu(hd[:, :D_FF]) * hd[:, D_FF:]).astype(BF16)
    f = _dot(a, wout_ref[...])
    o_ref[...] = h + _rms(f, gpost_ref[...])


def _ffn_kernel(h_ref, gpre_ref, gpost_ref, win_ref, wout_ref, o_ref):
    _ffn_body(h_ref[...], gpre_ref, gpost_ref, win_ref, wout_ref, o_ref)


def _oproj_ffn_kernel(h_ref, att_ref, wo_ref, gmix_ref, gpre_ref, gpost_ref, win_ref, wout_ref, o_ref):
    h = h_ref[...] + _rms(_dot(att_ref[...], wo_ref[...]), gmix_ref[...])
    _ffn_body(h, gpre_ref, gpost_ref, win_ref, wout_ref, o_ref)


def _ffn_layer(h, g_pre, g_post, w_in, w_out, att=None, w_o=None, g_mix=None):
    seq = h.shape[0]
    row_spec = pl.BlockSpec((FFN_ROWS, D_MODEL), lambda i: (i, 0))
    tail = (g_pre.reshape(1, D_MODEL), g_post.reshape(1, D_MODEL), w_in.astype(BF16), w_out.astype(BF16))
    if att is None:
        kern, args, specs = _ffn_kernel, (h,) + tail, [row_spec]
    else:
        head = (att, w_o.astype(BF16), g_mix.reshape(1, D_MODEL))
        kern, args = _oproj_ffn_kernel, (h,) + head + tail
        specs = [row_spec, row_spec, _const_spec(head[1].shape), _const_spec(head[2].shape)]
    specs = specs + [_const_spec(a.shape) for a in tail]
    return pl.pallas_call(
        kern,
        name="ffn",
        grid=(seq // FFN_ROWS,),
        in_specs=specs,
        out_specs=row_spec,
        out_shape=jax.ShapeDtypeStruct((seq, D_MODEL), F32),
        compiler_params=pltpu.CompilerParams(
            dimension_semantics=("arbitrary",), vmem_limit_bytes=_vmem_limit(56 * 1024 * 1024)),
    )(*args)


def _q_of(h, gq_ref, wq_ref):
    u = _rms(h, gq_ref[...]).astype(BF16)
    return (_dot(u, wq_ref[...]) * (HEAD_DIM ** -0.5 * LOG2_E)).astype(BF16)


def _qkv_kernel(h_ref, gq_ref, wq_ref, gkv_ref, wkt_ref, wv_ref, q_ref, kt_ref, v_ref):
    h = h_ref[...]
    q_ref[...] = _q_of(h, gq_ref, wq_ref)
    hk = _rms(h, gkv_ref[...]).astype(BF16)
    kt_ref[...] = lax.dot_general(wkt_ref[...], hk, (((1,), (1,)), ((), ())),
                                  preferred_element_type=F32).astype(BF16)
    v_ref[...] = _dot(hk, wv_ref[...]).astype(BF16)


def _q_kernel(h_ref, gq_ref, wq_ref, q_ref):
    q_ref[...] = _q_of(h_ref[...], gq_ref, wq_ref)


def _qkv_proj(h, g_q, w_q, g_kv, w_k, w_v):
    seq = h.shape[0]
    row_spec = pl.BlockSpec((PROJ_ROWS, D_MODEL), lambda i: (i, 0))
    col_spec = pl.BlockSpec((D_MODEL, PROJ_ROWS), lambda i: (0, i))
    args = (h, g_q.reshape(1, D_MODEL), w_q.astype(BF16), g_kv.reshape(1, D_MODEL),
            w_k.T.astype(BF16), w_v.astype(BF16))
    return pl.pallas_call(
        _qkv_kernel,
        name="qkv_proj",
        grid=(seq // PROJ_ROWS,),
        in_specs=[row_spec] + [_const_spec(a.shape) for a in args[1:]],
        out_specs=[row_spec, col_spec, row_spec],
        out_shape=[jax.ShapeDtypeStruct((seq, D_MODEL), BF16),
                   jax.ShapeDtypeStruct((D_MODEL, seq), BF16),
                   jax.ShapeDtypeStruct((seq, D_MODEL), BF16)],
        compiler_params=pltpu.CompilerParams(
            dimension_semantics=("arbitrary",), vmem_limit_bytes=_vmem_limit(40 * 1024 * 1024)),
    )(*args)


def _q_proj(h, g_q, w_q):
    seq = h.shape[0]
    row_spec = pl.BlockSpec((PROJ_ROWS, D_MODEL), lambda i: (i, 0))
    args = (h, g_q.reshape(1, D_MODEL), w_q.astype(BF16))
    return pl.pallas_call(
        _q_kernel,
        name="q_proj",
        grid=(seq // PROJ_ROWS,),
        in_specs=[row_spec] + [_const_spec(a.shape) for a in args[1:]],
        out_specs=row_spec,
        out_shape=jax.ShapeDtypeStruct((seq, D_MODEL), BF16),
        compiler_params=pltpu.CompilerParams(
            dimension_semantics=("arbitrary",), vmem_limit_bytes=_vmem_limit(40 * 1024 * 1024)),
    )(*args)


def _attn_kernel(q_ref, kt_ref, v_ref, o_ref, qm_ref, acc_ref, lr_ref):
    tile = pl.program_id(1)
    lane = lax.broadcasted_iota(jnp.int32, (ATT_KEYS, V7X_LANES), 1)
    head0 = lane < HEAD_DIM
    srow = lax.broadcasted_iota(jnp.int32, (ATT_PAIR_ROWS, ATT_KEYS), 0)
    skey = lax.broadcasted_iota(jnp.int32, (ATT_PAIR_ROWS, ATT_KEYS), 1)
    causal = jnp.where(srow >= ATT_KEYS, srow - ATT_KEYS, srow) > skey
    trow = lax.broadcasted_iota(jnp.int32, (2 * ATT_KEYS, 2 * ATT_KEYS), 0)
    tcol = lax.broadcasted_iota(jnp.int32, (2 * ATT_KEYS, 2 * ATT_KEYS), 1)
    tkey = jnp.where(trow >= ATT_KEYS, trow - ATT_KEYS, trow)
    tri_ones = jnp.where((tkey >= tcol) | (tcol >= ATT_KEYS), 1.0, 0.0).astype(BF16)

    for qb in range(ATT_QBLOCKS):
        q = q_ref[qb * ATT_KEYS:(qb + 1) * ATT_KEYS, :]
        zero = jnp.zeros_like(q)
        qm_ref[qb] = jnp.concatenate([jnp.where(head0, q, zero), jnp.where(head0, zero, q)], axis=0)

    half = ATT_KEYS // 2
    early = [slice(hh * ATT_KEYS, hh * ATT_KEYS + half) for hh in range(HEADS_PER_STEP)]
    late = [slice(hh * ATT_KEYS + half, (hh + 1) * ATT_KEYS) for hh in range(HEADS_PER_STEP)]

    def pick(ref, qb, parts):
        return ref[qb] if parts is None else jnp.concatenate([ref[qb, p, :] for p in parts], axis=0)

    def step(kv_blocks, keep, state, parts=None):
        nrows = ATT_PAIR_ROWS if parts is None else ATT_PAIR_ROWS // 2
        starts = [pl.multiple_of(kb * ATT_KEYS, ATT_KEYS) for kb in kv_blocks]
        zs = [_dot(pick(qm_ref, qb, parts), kt_ref[:, pl.ds(starts[qb], ATT_KEYS)]) for qb in range(ATT_QBLOCKS)]
        sps = []
        for qb in range(ATT_QBLOCKS):
            sp = jnp.maximum(zs[qb], 0.0) + jnp.log2(1.0 + jnp.exp2(_neg_abs(zs[qb])))
            sps.append(sp if keep is None else jnp.where(keep[qb], sp, 0.0))
        sp = jnp.concatenate(sps, axis=0)
        sp_hi = sp.astype(BF16)
        sp_lo = (sp - sp_hi.astype(F32)).astype(BF16)
        sums = _dot(jnp.concatenate([sp_hi, sp_lo], axis=1), tri_ones)
        out = []
        for qb in range(ATT_QBLOCKS):
            rows = slice(qb * nrows, (qb + 1) * nrows)
            logw = zs[qb] - sums[rows, :ATT_KEYS]
            total = sums[rows, ATT_KEYS:]
            if state is not None:
                acc, log_rest = state[qb]
                logw = logw + log_rest
            w = jnp.exp2(logw)
            if keep is not None:
                w = jnp.where(keep[qb], w, 0.0)
            pv = _dot(w.astype(BF16), v_ref[pl.ds(starts[qb], ATT_KEYS), :])
            out.append((pv, -total) if state is None else (acc + pv, log_rest - total))
        return out

    def load_state(parts=None):
        return [(pick(acc_ref, qb, parts), pick(lr_ref, qb, parts)) for qb in range(ATT_QBLOCKS)]

    def store_state(state):
        top = None
        for qb, (acc, log_rest) in enumerate(state):
            acc_ref[qb] = acc
            lr_ref[qb] = log_rest
            top = log_rest if top is None else jnp.maximum(top, log_rest)
        top_early = jnp.max(jnp.maximum(top[early[0]], top[early[1]]))
        top_late = jnp.max(jnp.maximum(top[late[0]], top[late[1]]))
        return top_early, top_late

    def store_early(state):
        top = None
        for qb, (acc, log_rest) in enumerate(state):
            for n, p in enumerate(early):
                acc_ref[qb, p, :] = acc[n * half:(n + 1) * half]
                lr_ref[qb, p, :] = log_rest[n * half:(n + 1) * half]
            top = log_rest if top is None else jnp.maximum(top, log_rest)
        return jnp.max(top)

    first_block = tile * ATT_QBLOCKS
    state = step([first_block + qb for qb in range(ATT_QBLOCKS)], [causal] * ATT_QBLOCKS, None)
    before = [first_block + qb - 1 for qb in range(ATT_QBLOCKS)]
    tops = store_state(step([jnp.maximum(kb, 0) for kb in before], [kb >= 0 for kb in before], state))

    def walk_full(carry):
        s = carry[0]
        return (s + 1,) + store_state(step([first_block + qb - s for qb in range(ATT_QBLOCKS)], None, load_state()))

    def walk_early(carry):
        s, _, top_late = carry
        state = step([first_block + qb - s for qb in range(ATT_QBLOCKS)], None, load_state(early), early)
        return s + 1, store_early(state), top_late

    def walk_ragged(carry):
        s = carry[0]
        blocks = [first_block + qb - s for qb in range(ATT_QBLOCKS)]
        state = step([jnp.maximum(kb, 0) for kb in blocks], [kb >= 0 for kb in blocks], load_state())
        return (s + 1,) + store_state(state)

    carry = (jnp.int32(2),) + tops
    carry = lax.while_loop(lambda c: (c[0] <= first_block) & (c[2] > ATT_LOG2_CUTOFF), walk_full, carry)
    carry = lax.while_loop(lambda c: (c[0] <= first_block) & (c[1] > ATT_LOG2_CUTOFF), walk_early, carry)
    lax.while_loop(lambda c: (c[0] < first_block + ATT_QBLOCKS) & (jnp.maximum(c[1], c[2]) > ATT_LOG2_CUTOFF),
                   walk_ragged, carry)

    for qb in range(ATT_QBLOCKS):
        acc = acc_ref[qb]
        out = jnp.where(head0, acc[:ATT_KEYS], acc[ATT_KEYS:])
        o_ref[qb * ATT_KEYS:(qb + 1) * ATT_KEYS, :] = out.astype(BF16)


def _attention(q, kt, v):
    seq = q.shape[0]
    assert HEADS_PER_STEP == 2
    state = pltpu.VMEM((ATT_QBLOCKS, ATT_PAIR_ROWS, ATT_KEYS), F32)
    stacked_q = pltpu.VMEM((ATT_QBLOCKS, ATT_PAIR_ROWS, V7X_LANES), BF16)
    return pl.pallas_call(
        _attn_kernel,
        name="stickbreaking_attention",
        grid=(N_HEADS // HEADS_PER_STEP, seq // ATT_ROWS),
        in_specs=[pl.BlockSpec((ATT_ROWS, V7X_LANES), lambda p, i: (i, p)),
                  pl.BlockSpec((V7X_LANES, seq), lambda p, i: (p, 0)),
                  pl.BlockSpec((seq, V7X_LANES), lambda p, i: (0, p))],
        out_specs=pl.BlockSpec((ATT_ROWS, V7X_LANES), lambda p, i: (i, p)),
        out_shape=jax.ShapeDtypeStruct((seq, D_MODEL), BF16),
        scratch_shapes=[stacked_q, state, state],
        compiler_params=pltpu.CompilerParams(
            dimension_semantics=("arbitrary", "arbitrary"), vmem_limit_bytes=_vmem_limit(40 * 1024 * 1024)),
    )(q, kt, v)


def kernel(x, norm_mix_pre, norm_mix_post, norm_ffn_pre, norm_ffn_post, w_ffn_in, w_ffn_out, s5_log_dt, s5_a_re, s5_a_im, s5_b_re, s5_b_im, s5_c_re, s5_c_im, s5_d, s5_w_glu, s5_b_glu, kv_norm, w_k, w_v, w_q, w_o):
    bsz, seq, _ = x.shape
    assert bsz == 1 and seq % S5_ROWS == 0 and seq % ATT_ROWS == 0 and seq % FFN_ROWS == 0
    blocks = seq // S5_ROWS
    h = x.reshape(blocks, V7X_SUBLANES, S5_SEG, D_MODEL).transpose(0, 2, 1, 3).reshape(seq, D_MODEL)
    s5_params = jax.vmap(_s5_params)(s5_log_dt, s5_a_re, s5_a_im, s5_b_re, s5_b_im, s5_c_re, s5_c_im)
    for layer in range(N_A_LAYERS):
        params = tuple(p[layer] for p in s5_params)
        h = _s5_layer(h, norm_mix_pre[layer], norm_mix_post[layer], params, s5_d[layer],
                      s5_w_glu[layer], s5_b_glu[layer])
        h = _ffn_layer(h, norm_ffn_pre[layer], norm_ffn_post[layer], w_ffn_in[layer], w_ffn_out[layer])
    h = h.reshape(blocks, S5_SEG, V7X_SUBLANES, D_MODEL).transpose(0, 2, 1, 3).reshape(seq, D_MODEL)

    kt = v = None
    for layer in range(N_A_LAYERS, DEPTH):
        b = layer - N_A_LAYERS
        if b == 0:
            q, kt, v = _qkv_proj(h, norm_mix_pre[layer], w_q[b], kv_norm, w_k, w_v)
        else:
            q = _q_proj(h, norm_mix_pre[layer], w_q[b])
        att = _attention(q, kt, v)
        h = _ffn_layer(h, norm_ffn_pre[layer], norm_ffn_post[layer], w_ffn_in[layer], w_ffn_out[layer],
                       att=att, w_o=w_o[b], g_mix=norm_mix_post[layer])
    return h.reshape(bsz, seq, D_MODEL)
```

```python
import functools
import math

import jax
import jax.numpy as jnp
from jax import lax
from jax.experimental import pallas as pl
from jax.experimental.pallas import tpu as pltpu

F32 = jnp.float32
BF16 = jnp.bfloat16

D_MODEL = 1024
DEPTH = 4
N_A_LAYERS = DEPTH // 2
SSM_GROUP = 16
SSM_GROUPS = D_MODEL // SSM_GROUP
SSM_STATE = 64
N_HEADS = 16
HEAD_DIM = D_MODEL // N_HEADS
D_FF = 2816
EPS = 1e-6

V7X_LANES = 128
V7X_SUBLANES = 8
BF16_ROWS = 2 * V7X_SUBLANES
V7X_MXU_DIM = 256
V7X_VMEM_BYTES = 64 * 1024 * 1024

S5_KTILES = D_MODEL // V7X_MXU_DIM
S5_TILE_GROUPS = V7X_MXU_DIM // SSM_GROUP
S5_TILE_STATES = S5_TILE_GROUPS * SSM_STATE
S5_SEG = 64
S5_ROWS = V7X_SUBLANES * S5_SEG
S5_COLS = 512

FFN_ROWS = 512
PROJ_ROWS = 512

ATT_KEYS = 128
ATT_QBLOCKS = 8
ATT_ROWS = ATT_QBLOCKS * ATT_KEYS
HEADS_PER_STEP = V7X_LANES // HEAD_DIM
ATT_PAIR_ROWS = HEADS_PER_STEP * ATT_KEYS
LOG2_E = math.log2(math.e)
ATT_LOG2_CUTOFF = -104.0 * LOG2_E


def _vmem_limit(nbytes):
    return int(min(nbytes, V7X_VMEM_BYTES - 6 * 1024 * 1024))


def _rms(x, g):
    return x * lax.rsqrt(jnp.mean(x * x, axis=-1, keepdims=True) + EPS) * g


def _dot(a, b):
    return jnp.dot(a, b, preferred_element_type=F32)


def _neg_abs(x):
    return -jnp.abs(x)


def _const_spec(shape):
    zeros = (0,) * len(shape)
    return pl.BlockSpec(shape, lambda *_: zeros, pipeline_mode=pl.Buffered(1))


def _s5_kernel(h_ref, gpre_ref, gpost_ref, wb_ref, wc_ref, lre_ref, lim_ref, mre_ref, mim_ref,
               d_ref, wglu_ref, bglu_ref, o_ref,
               xre_ref, xim_ref, xb_ref, cre_ref, cim_ref, y_ref):
    @pl.when(pl.program_id(0) == 0)
    def _():
        cre_ref[...] = jnp.zeros_like(cre_ref)
        cim_ref[...] = jnp.zeros_like(cim_ref)

    h = h_ref[...]
    u = _rms(h, gpre_ref[...])
    u_bf = u.astype(BF16)
    zeros = jnp.zeros((V7X_SUBLANES, S5_COLS), F32)
    row = lax.broadcasted_iota(jnp.int32, (V7X_SUBLANES, S5_COLS), 0)

    for kt in range(S5_KTILES):
        slot = kt % 2
        bu = _dot(u_bf[:, kt * V7X_MXU_DIM:(kt + 1) * V7X_MXU_DIM], wb_ref[kt])
        xre_ref[slot] = bu[:, :S5_TILE_STATES]
        xim_ref[slot] = bu[:, S5_TILE_STATES:]

        for cg in range(S5_TILE_STATES // S5_COLS):
            cs = slice(cg * S5_COLS, (cg + 1) * S5_COLS)
            lre = jnp.broadcast_to(lre_ref[kt, :, cs], (V7X_SUBLANES, S5_COLS))
            lim = jnp.broadcast_to(lim_ref[kt, :, cs], (V7X_SUBLANES, S5_COLS))

            er, ei = zeros, zeros
            for i in range(S5_SEG):
                rows = pl.ds(i * V7X_SUBLANES, V7X_SUBLANES)
                er, ei = (lre * er - lim * ei + xre_ref[slot, rows, cs],
                          lre * ei + lim * er + xim_ref[slot, rows, cs])
                xre_ref[slot, rows, cs] = er
                xim_ref[slot, rows, cs] = ei

            mre = mre_ref[kt, :, cs]
            mim = mim_ref[kt, :, cs]
            cur_r = cre_ref[kt, :, cs]
            cur_i = cim_ref[kt, :, cs]
            xin_r = zeros
            xin_i = zeros
            for r in range(V7X_SUBLANES):
                xin_r = jnp.where(row == r, cur_r, xin_r)
                xin_i = jnp.where(row == r, cur_i, xin_i)
                nxt_r = mre * cur_r - mim * cur_i + er[r:r + 1, :]
                nxt_i = mre * cur_i + mim * cur_r + ei[r:r + 1, :]
                cur_r, cur_i = nxt_r, nxt_i
            cre_ref[kt, :, cs] = cur_r
            cim_ref[kt, :, cs] = cur_i

            cs_im = slice(S5_TILE_STATES + cg * S5_COLS, S5_TILE_STATES + (cg + 1) * S5_COLS)

            cr, ci = xin_r, xin_i
            for j in range(S5_ROWS // BF16_ROWS):
                full_r, full_i = [], []
                for half in range(BF16_ROWS // V7X_SUBLANES):
                    cr, ci = lre * cr - lim * ci, lre * ci + lim * cr
                    rows = pl.ds(j * BF16_ROWS + half * V7X_SUBLANES, V7X_SUBLANES)
                    full_r.append(xre_ref[slot, rows, cs] + cr)
                    full_i.append(xim_ref[slot, rows, cs] + ci)
                packed = pl.ds(j * BF16_ROWS, BF16_ROWS)
                xb_ref[slot, packed, cs] = jnp.concatenate(full_r, axis=0).astype(BF16)
                xb_ref[slot, packed, cs_im] = jnp.concatenate(full_i, axis=0).astype(BF16)

        y_ref[:, kt * V7X_MXU_DIM:(kt + 1) * V7X_MXU_DIM] = _dot(xb_ref[slot], wc_ref[kt])

    y = y_ref[...] + d_ref[...] * u
    a = jax.nn.gelu(y).astype(BF16)
    z = _dot(a, wglu_ref[...]) + bglu_ref[...]
    mix = z[:, :D_MODEL] * jax.nn.sigmoid(z[:, D_MODEL:])
    o_ref[...] = h + _rms(mix, gpost_ref[...])


def _s5_params(log_dt, a_re, a_im, b_re, b_im, c_re, c_im):
    dt = jnp.exp(log_dt.astype(F32))[:, None]
    a_re = a_re.astype(F32)
    a_im = a_im.astype(F32)
    mag = jnp.exp(a_re * dt)
    lam_re = mag * jnp.cos(a_im * dt)
    lam_im = mag * jnp.sin(a_im * dt)
    den = a_re * a_re + a_im * a_im
    nr = lam_re - 1.0
    coef_re = (nr * a_re + lam_im * a_im) / den
    coef_im = (lam_im * a_re - nr * a_im) / den
    bb_re = coef_re[..., None] * b_re - coef_im[..., None] * b_im
    bb_im = coef_re[..., None] * b_im + coef_im[..., None] * b_re

    eye = jnp.eye(S5_TILE_GROUPS, dtype=F32)

    def in_proj(b):
        b = b.reshape(S5_KTILES, S5_TILE_GROUPS, SSM_STATE, SSM_GROUP)
        w = jnp.einsum('kgpc,gh->kgchp', b, eye)
        return w.reshape(S5_KTILES, V7X_MXU_DIM, S5_TILE_STATES)

    def out_proj(c):
        c = c.astype(F32).reshape(S5_KTILES, S5_TILE_GROUPS, SSM_GROUP, SSM_STATE)
        w = jnp.einsum('kgcp,gh->kgphc', c, eye)
        return w.reshape(S5_KTILES, S5_TILE_STATES, V7X_MXU_DIM)

    wb = jnp.concatenate([in_proj(bb_re), in_proj(bb_im)], axis=2).astype(BF16)
    wc = jnp.concatenate([out_proj(c_re), -out_proj(c_im)], axis=1).astype(BF16)

    def tile(v):
        return v.reshape(S5_KTILES, 1, S5_TILE_STATES)

    seg_mag = jnp.exp(float(S5_SEG) * a_re * dt)
    m_re = seg_mag * jnp.cos(float(S5_SEG) * a_im * dt)
    m_im = seg_mag * jnp.sin(float(S5_SEG) * a_im * dt)
    return wb, wc, tile(lam_re), tile(lam_im), tile(m_re), tile(m_im)


def _s5_layer(h, g_pre, g_post, params, d_skip, w_glu, b_glu):
    seq = h.shape[0]
    row_spec = pl.BlockSpec((S5_ROWS, D_MODEL), lambda i: (i, 0))
    args = (h, g_pre.reshape(1, D_MODEL), g_post.reshape(1, D_MODEL),
            *params, d_skip.reshape(1, D_MODEL).astype(F32), w_glu.astype(BF16),
            b_glu.reshape(1, 2 * D_MODEL).astype(F32))
    in_specs = [row_spec] + [_const_spec(a.shape) for a in args[1:]]
    return pl.pallas_call(
        _s5_kernel,
        name="s5_mixer",
        grid=(seq // S5_ROWS,),
        in_specs=in_specs,
        out_specs=row_spec,
        out_shape=jax.ShapeDtypeStruct((seq, D_MODEL), F32),
        scratch_shapes=[
            pltpu.VMEM((2, S5_ROWS, S5_TILE_STATES), F32),
            pltpu.VMEM((2, S5_ROWS, S5_TILE_STATES), F32),
            pltpu.VMEM((2, S5_ROWS, 2 * S5_TILE_STATES), BF16),
            pltpu.VMEM((S5_KTILES, 1, S5_TILE_STATES), F32),
            pltpu.VMEM((S5_KTILES, 1, S5_TILE_STATES), F32),
            pltpu.VMEM((S5_ROWS, D_MODEL), F32),
        ],
        compiler_params=pltpu.CompilerParams(
            dimension_semantics=("arbitrary",), vmem_limit_bytes=_vmem_limit(56 * 1024 * 1024)),
    )(*args)


def _ffn_body(h, gpre_ref, gpost_ref, win_ref, wout_ref, o_ref):
    xn = _rms(h, gpre_ref[...]).astype(BF16)
    hd = _dot(xn, win_ref[...])
    a = (jax.nn.silu(hd[:, :D_FF]) * hd[:, D_FF:]).astype(BF16)
    f = _dot(a, wout_ref[...])
    o_ref[...] = h + _rms(f, gpost_ref[...])


def _ffn_kernel(h_ref, gpre_ref, gpost_ref, win_ref, wout_ref, o_ref):
    _ffn_body(h_ref[...], gpre_ref, gpost_ref, win_ref, wout_ref, o_ref)


def _oproj_ffn_kernel(h_ref, att_ref, wo_ref, gmix_ref, gpre_ref, gpost_ref, win_ref, wout_ref, o_ref):
    h = h_ref[...] + _rms(_dot(att_ref[...], wo_ref[...]), gmix_ref[...])
    _ffn_body(h, gpre_ref, gpost_ref, win_ref, wout_ref, o_ref)


def _ffn_layer(h, g_pre, g_post, w_in, w_out, att=None, w_o=None, g_mix=None):
    seq = h.shape[0]
    row_spec = pl.BlockSpec((FFN_ROWS, D_MODEL), lambda i: (i, 0))
    tail = (g_pre.reshape(1, D_MODEL), g_post.reshape(1, D_MODEL), w_in.astype(BF16), w_out.astype(BF16))
    if att is None:
        kern, args, specs = _ffn_kernel, (h,) + tail, [row_spec]
    else:
        head = (att, w_o.astype(BF16), g_mix.reshape(1, D_MODEL))
        kern, args = _oproj_ffn_kernel, (h,) + head + tail
        specs = [row_spec, row_spec, _const_spec(head[1].shape), _const_spec(head[2].shape)]
    specs = specs + [_const_spec(a.shape) for a in tail]
    return pl.pallas_call(
        kern,
        name="ffn",
        grid=(seq // FFN_ROWS,),
        in_specs=specs,
        out_specs=row_spec,
        out_shape=jax.ShapeDtypeStruct((seq, D_MODEL), F32),
        compiler_params=pltpu.CompilerParams(
            dimension_semantics=("arbitrary",), vmem_limit_bytes=_vmem_limit(56 * 1024 * 1024)),
    )(*args)


def _q_of(h, gq_ref, wq_ref):
    u = _rms(h, gq_ref[...]).astype(BF16)
    return (_dot(u, wq_ref[...]) * (HEAD_DIM ** -0.5 * LOG2_E)).astype(BF16)


def _qkv_kernel(h_ref, gq_ref, wq_ref, gkv_ref, wkt_ref, wv_ref, q_ref, kt_ref, v_ref):
    h = h_ref[...]
    q_ref[...] = _q_of(h, gq_ref, wq_ref)
    hk = _rms(h, gkv_ref[...]).astype(BF16)
    kt_ref[...] = lax.dot_general(wkt_ref[...], hk, (((1,), (1,)), ((), ())),
                                  preferred_element_type=F32).astype(BF16)
    v_ref[...] = _dot(hk, wv_ref[...]).astype(BF16)


def _q_kernel(h_ref, gq_ref, wq_ref, q_ref):
    q_ref[...] = _q_of(h_ref[...], gq_ref, wq_ref)


def _qkv_proj(h, g_q, w_q, g_kv, w_k, w_v):
    seq = h.shape[0]
    row_spec = pl.BlockSpec((PROJ_ROWS, D_MODEL), lambda i: (i, 0))
    col_spec = pl.BlockSpec((D_MODEL, PROJ_ROWS), lambda i: (0, i))
    args = (h, g_q.reshape(1, D_MODEL), w_q.astype(BF16), g_kv.reshape(1, D_MODEL),
            w_k.T.astype(BF16), w_v.astype(BF16))
    return pl.pallas_call(
        _qkv_kernel,
        name="qkv_proj",
        grid=(seq // PROJ_ROWS,),
        in_specs=[row_spec] + [_const_spec(a.shape) for a in args[1:]],
        out_specs=[row_spec, col_spec, row_spec],
        out_shape=[jax.ShapeDtypeStruct((seq, D_MODEL), BF16),
                   jax.ShapeDtypeStruct((D_MODEL, seq), BF16),
                   jax.ShapeDtypeStruct((seq, D_MODEL), BF16)],
        compiler_params=pltpu.CompilerParams(
            dimension_semantics=("arbitrary",), vmem_limit_bytes=_vmem_limit(40 * 1024 * 1024)),
    )(*args)


def _q_proj(h, g_q, w_q):
    seq = h.shape[0]
    row_spec = pl.BlockSpec((PROJ_ROWS, D_MODEL), lambda i: (i, 0))
    args = (h, g_q.reshape(1, D_MODEL), w_q.astype(BF16))
    return pl.pallas_call(
        _q_kernel,
        name="q_proj",
        grid=(seq // PROJ_ROWS,),
        in_specs=[row_spec] + [_const_spec(a.shape) for a in args[1:]],
        out_specs=row_spec,
        out_shape=jax.ShapeDtypeStruct((seq, D_MODEL), BF16),
        compiler_params=pltpu.CompilerParams(
            dimension_semantics=("arbitrary",), vmem_limit_bytes=_vmem_limit(40 * 1024 * 1024)),
    )(*args)


def _attn_kernel(q_ref, kt_ref, v_ref, o_ref, qm_ref, acc_ref, lr_ref):
    tile = pl.program_id(1)
    lane = lax.broadcasted_iota(jnp.int32, (ATT_KEYS, V7X_LANES), 1)
    head0 = lane < HEAD_DIM
    srow = lax.broadcasted_iota(jnp.int32, (ATT_PAIR_ROWS, ATT_KEYS), 0)
    skey = lax.broadcasted_iota(jnp.int32, (ATT_PAIR_ROWS, ATT_KEYS), 1)
    causal = jnp.where(srow >= ATT_KEYS, srow - ATT_KEYS, srow) > skey
    trow = lax.broadcasted_iota(jnp.int32, (ATT_KEYS, 2 * ATT_KEYS), 0)
    tcol = lax.broadcasted_iota(jnp.int32, (ATT_KEYS, 2 * ATT_KEYS), 1)
    tri_ones = jnp.where((trow >= tcol) | (tcol >= ATT_KEYS), 1.0, 0.0).astype(BF16)

    for qb in range(ATT_QBLOCKS):
        q = q_ref[qb * ATT_KEYS:(qb + 1) * ATT_KEYS, :]
        zero = jnp.zeros_like(q)
        qm_ref[qb] = jnp.concatenate([jnp.where(head0, q, zero), jnp.where(head0, zero, q)], axis=0)

    half = ATT_KEYS // 2
    early = [slice(hh * ATT_KEYS, hh * ATT_KEYS + half) for hh in range(HEADS_PER_STEP)]
    late = [slice(hh * ATT_KEYS + half, (hh + 1) * ATT_KEYS) for hh in range(HEADS_PER_STEP)]

    def pick(ref, qb, parts):
        return ref[qb] if parts is None else jnp.concatenate([ref[qb, p, :] for p in parts], axis=0)

    def step(kv_blocks, keep, state, parts=None):
        nrows = ATT_PAIR_ROWS if parts is None else ATT_PAIR_ROWS // 2
        starts = [pl.multiple_of(kb * ATT_KEYS, ATT_KEYS) for kb in kv_blocks]
        zs = [_dot(pick(qm_ref, qb, parts), kt_ref[:, pl.ds(starts[qb], ATT_KEYS)]) for qb in range(ATT_QBLOCKS)]
        sps = []
        for qb in range(ATT_QBLOCKS):
            sp = jnp.maximum(zs[qb], 0.0) + jnp.log2(1.0 + jnp.exp2(_neg_abs(zs[qb])))
            sps.append(sp if keep is None else jnp.where(keep[qb], sp, 0.0))
        sums = _dot(jnp.concatenate(sps, axis=0).astype(BF16), tri_ones)
        out = []
        for qb in range(ATT_QBLOCKS):
            rows = slice(qb * nrows, (qb + 1) * nrows)
            logw = zs[qb] - sums[rows, :ATT_KEYS]
            total = sums[rows, ATT_KEYS:]
            if state is not None:
                acc, log_rest = state[qb]
                logw = logw + log_rest
            w = jnp.exp2(logw)
            if keep is not None:
                w = jnp.where(keep[qb], w, 0.0)
            pv = _dot(w.astype(BF16), v_ref[pl.ds(starts[qb], ATT_KEYS), :])
            out.append((pv, -total) if state is None else (acc + pv, log_rest - total))
        return out

    def load_state(parts=None):
        return [(pick(acc_ref, qb, parts), pick(lr_ref, qb, parts)) for qb in range(ATT_QBLOCKS)]

    def store_state(state):
        top = None
        for qb, (acc, log_rest) in enumerate(state):
            acc_ref[qb] = acc
            lr_ref[qb] = log_rest
            top = log_rest if top is None else jnp.maximum(top, log_rest)
        top_early = jnp.max(jnp.maximum(top[early[0]], top[early[1]]))
        top_late = jnp.max(jnp.maximum(top[late[0]], top[late[1]]))
        return top_early, top_late

    def store_early(state):
        top = None
        for qb, (acc, log_rest) in enumerate(state):
            for n, p in enumerate(early):
                acc_ref[qb, p, :] = acc[n * half:(n + 1) * half]
                lr_ref[qb, p, :] = log_rest[n * half:(n + 1) * half]
            top = log_rest if top is None else jnp.maximum(top, log_rest)
        return jnp.max(top)

    first_block = tile * ATT_QBLOCKS
    state = step([first_block + qb for qb in range(ATT_QBLOCKS)], [causal] * ATT_QBLOCKS, None)
    before = [first_block + qb - 1 for qb in range(ATT_QBLOCKS)]
    tops = store_state(step([jnp.maximum(kb, 0) for kb in before], [kb >= 0 for kb in before], state))

    def walk_full(carry):
        s = carry[0]
        return (s + 1,) + store_state(step([first_block + qb - s for qb in range(ATT_QBLOCKS)], None, load_state()))

    def walk_early(carry):
        s, _, top_late = carry
        state = step([first_block + qb - s for qb in range(ATT_QBLOCKS)], None, load_state(early), early)
        return s + 1, store_early(state), top_late

    def walk_ragged(carry):
        s = carry[0]
        blocks = [first_block + qb - s for qb in range(ATT_QBLOCKS)]
        state = step([jnp.maximum(kb, 0) for kb in blocks], [kb >= 0 for kb in blocks], load_state())
        return (s + 1,) + store_state(state)

    carry = (jnp.int32(2),) + tops
    carry = lax.while_loop(lambda c: (c[0] <= first_block) & (c[2] > ATT_LOG2_CUTOFF), walk_full, carry)
    carry = lax.while_loop(lambda c: (c[0] <= first_block) & (c[1] > ATT_LOG2_CUTOFF), walk_early, carry)
    lax.while_loop(lambda c: (c[0] < first_block + ATT_QBLOCKS) & (jnp.maximum(c[1], c[2]) > ATT_LOG2_CUTOFF),
                   walk_ragged, carry)

    for qb in range(ATT_QBLOCKS):
        acc = acc_ref[qb]
        out = jnp.where(head0, acc[:ATT_KEYS], acc[ATT_KEYS:])
        o_ref[qb * ATT_KEYS:(qb + 1) * ATT_KEYS, :] = out.astype(BF16)


def _attention(q, kt, v):
    seq = q.shape[0]
    assert HEADS_PER_STEP == 2
    state = pltpu.VMEM((ATT_QBLOCKS, ATT_PAIR_ROWS, ATT_KEYS), F32)
    stacked_q = pltpu.VMEM((ATT_QBLOCKS, ATT_PAIR_ROWS, V7X_LANES), BF16)
    return pl.pallas_call(
        _attn_kernel,
        name="stickbreaking_attention",
        grid=(N_HEADS // HEADS_PER_STEP, seq // ATT_ROWS),
        in_specs=[pl.BlockSpec((ATT_ROWS, V7X_LANES), lambda p, i: (i, p)),
                  pl.BlockSpec((V7X_LANES, seq), lambda p, i: (p, 0)),
                  pl.BlockSpec((seq, V7X_LANES), lambda p, i: (0, p))],
        out_specs=pl.BlockSpec((ATT_ROWS, V7X_LANES), lambda p, i: (i, p)),
        out_shape=jax.ShapeDtypeStruct((seq, D_MODEL), BF16),
        scratch_shapes=[stacked_q, state, state],
        compiler_params=pltpu.CompilerParams(
            dimension_semantics=("arbitrary", "arbitrary"), vmem_limit_bytes=_vmem_limit(40 * 1024 * 1024)),
    )(q, kt, v)


def kernel(x, norm_mix_pre, norm_mix_post, norm_ffn_pre, norm_ffn_post, w_ffn_in, w_ffn_out, s5_log_dt, s5_a_re, s5_a_im, s5_b_re, s5_b_im, s5_c_re, s5_c_im, s5_d, s5_w_glu, s5_b_glu, kv_norm, w_k, w_v, w_q, w_o):
    bsz, seq, _ = x.shape
    assert bsz == 1 and seq % S5_ROWS == 0 and seq % ATT_ROWS == 0 and seq % FFN_ROWS == 0
    blocks = seq // S5_ROWS
    h = x.reshape(blocks, V7X_SUBLANES, S5_SEG, D_MODEL).transpose(0, 2, 1, 3).reshape(seq, D_MODEL)
    s5_params = jax.vmap(_s5_params)(s5_log_dt, s5_a_re, s5_a_im, s5_b_re, s5_b_im, s5_c_re, s5_c_im)
    for layer in range(N_A_LAYERS):
        params = tuple(p[layer] for p in s5_params)
        h = _s5_layer(h, norm_mix_pre[layer], norm_mix_post[layer], params, s5_d[layer],
                      s5_w_glu[layer], s5_b_glu[layer])
        h = _ffn_layer(h, norm_ffn_pre[layer], norm_ffn_post[layer], w_ffn_in[layer], w_ffn_out[layer])
    h = h.reshape(blocks, S5_SEG, V7X_SUBLANES, D_MODEL).transpose(0, 2, 1, 3).reshape(seq, D_MODEL)

    kt = v = None
    for layer in range(N_A_LAYERS, DEPTH):
        b = layer - N_A_LAYERS
        if b == 0:
            q, kt, v = _qkv_proj(h, norm_mix_pre[layer], w_q[b], kv_norm, w_k, w_v)
        else:
            q = _q_proj(h, norm_mix_pre[layer], w_q[b])
        att = _attention(q, kt, v)
        h = _ffn_layer(h, norm_ffn_pre[layer], norm_ffn_post[layer], w_ffn_in[layer], w_ffn_out[layer],
                       att=att, w_o=w_o[b], g_mix=norm_mix_post[layer])
    return h.reshape(bsz, seq, D_MODEL)
```

```python
import functools
import math

import jax
import jax.numpy as jnp
from jax import lax
from jax.experimental import pallas as pl
from jax.experimental.pallas import tpu as pltpu

F32 = jnp.float32
BF16 = jnp.bfloat16

D_MODEL = 1024
DEPTH = 4
N_A_LAYERS = DEPTH // 2
SSM_GROUP = 16
SSM_GROUPS = D_MODEL // SSM_GROUP
SSM_STATE = 64
N_HEADS = 16
HEAD_DIM = D_MODEL // N_HEADS
D_FF = 2816
EPS = 1e-6

V7X_LANES = 128
V7X_SUBLANES = 8
BF16_ROWS = 2 * V7X_SUBLANES
V7X_MXU_DIM = 256
V7X_VMEM_BYTES = 64 * 1024 * 1024

S5_KTILES = D_MODEL // V7X_MXU_DIM
S5_TILE_GROUPS = V7X_MXU_DIM // SSM_GROUP
S5_TILE_STATES = S5_TILE_GROUPS * SSM_STATE
S5_SEG = 64
S5_ROWS = V7X_SUBLANES * S5_SEG
S5_COLS = 512

FFN_ROWS = 512
PROJ_ROWS = 1024

ATT_KEYS = 128
ATT_QBLOCKS = 8
ATT_ROWS = ATT_QBLOCKS * ATT_KEYS
HEADS_PER_STEP = V7X_LANES // HEAD_DIM
ATT_PAIR_ROWS = HEADS_PER_STEP * ATT_KEYS
LOG2_E = math.log2(math.e)
ATT_LOG2_CUTOFF = -104.0 * LOG2_E


def _vmem_limit(nbytes):
    return int(min(nbytes, V7X_VMEM_BYTES - 6 * 1024 * 1024))


def _rms(x, g):
    return x * lax.rsqrt(jnp.mean(x * x, axis=-1, keepdims=True) + EPS) * g


def _dot(a, b):
    return jnp.dot(a, b, preferred_element_type=F32)


def _neg_abs(x):
    return -jnp.abs(x)


def _const_spec(shape):
    zeros = (0,) * len(shape)
    return pl.BlockSpec(shape, lambda *_: zeros, pipeline_mode=pl.Buffered(1))


def _s5_kernel(h_ref, gpre_ref, gpost_ref, wb_ref, wc_ref, lre_ref, lim_ref, mre_ref, mim_ref,
               d_ref, wglu_ref, bglu_ref, o_ref,
               xre_ref, xim_ref, xb_ref, cre_ref, cim_ref, y_ref):
    @pl.when(pl.program_id(0) == 0)
    def _():
        cre_ref[...] = jnp.zeros_like(cre_ref)
        cim_ref[...] = jnp.zeros_like(cim_ref)

    h = h_ref[...]
    u = _rms(h, gpre_ref[...])
    u_bf = u.astype(BF16)
    zeros = jnp.zeros((V7X_SUBLANES, S5_COLS), F32)
    row = lax.broadcasted_iota(jnp.int32, (V7X_SUBLANES, S5_COLS), 0)

    for kt in range(S5_KTILES):
        slot = kt % 2
        bu = _dot(u_bf[:, kt * V7X_MXU_DIM:(kt + 1) * V7X_MXU_DIM], wb_ref[kt])
        xre_ref[slot] = bu[:, :S5_TILE_STATES]
        xim_ref[slot] = bu[:, S5_TILE_STATES:]

        for cg in range(S5_TILE_STATES // S5_COLS):
            cs = slice(cg * S5_COLS, (cg + 1) * S5_COLS)
            lre = jnp.broadcast_to(lre_ref[kt, :, cs], (V7X_SUBLANES, S5_COLS))
            lim = jnp.broadcast_to(lim_ref[kt, :, cs], (V7X_SUBLANES, S5_COLS))

            er, ei = zeros, zeros
            for i in range(S5_SEG):
                rows = pl.ds(i * V7X_SUBLANES, V7X_SUBLANES)
                er, ei = (lre * er - lim * ei + xre_ref[slot, rows, cs],
                          lre * ei + lim * er + xim_ref[slot, rows, cs])
                xre_ref[slot, rows, cs] = er
                xim_ref[slot, rows, cs] = ei

            mre = mre_ref[kt, :, cs]
            mim = mim_ref[kt, :, cs]
            cur_r = cre_ref[kt, :, cs]
            cur_i = cim_ref[kt, :, cs]
            xin_r = zeros
            xin_i = zeros
            for r in range(V7X_SUBLANES):
                xin_r = jnp.where(row == r, cur_r, xin_r)
                xin_i = jnp.where(row == r, cur_i, xin_i)
                nxt_r = mre * cur_r - mim * cur_i + er[r:r + 1, :]
                nxt_i = mre * cur_i + mim * cur_r + ei[r:r + 1, :]
                cur_r, cur_i = nxt_r, nxt_i
            cre_ref[kt, :, cs] = cur_r
            cim_ref[kt, :, cs] = cur_i

            cs_im = slice(S5_TILE_STATES + cg * S5_COLS, S5_TILE_STATES + (cg + 1) * S5_COLS)

            cr, ci = xin_r, xin_i
            for j in range(S5_ROWS // BF16_ROWS):
                full_r, full_i = [], []
                for half in range(BF16_ROWS // V7X_SUBLANES):
                    cr, ci = lre * cr - lim * ci, lre * ci + lim * cr
                    rows = pl.ds(j * BF16_ROWS + half * V7X_SUBLANES, V7X_SUBLANES)
                    full_r.append(xre_ref[slot, rows, cs] + cr)
                    full_i.append(xim_ref[slot, rows, cs] + ci)
                packed = pl.ds(j * BF16_ROWS, BF16_ROWS)
                xb_ref[slot, packed, cs] = jnp.concatenate(full_r, axis=0).astype(BF16)
                xb_ref[slot, packed, cs_im] = jnp.concatenate(full_i, axis=0).astype(BF16)

        y_ref[:, kt * V7X_MXU_DIM:(kt + 1) * V7X_MXU_DIM] = _dot(xb_ref[slot], wc_ref[kt])

    y = y_ref[...] + d_ref[...] * u
    a = jax.nn.gelu(y).astype(BF16)
    z = _dot(a, wglu_ref[...]) + bglu_ref[...]
    mix = z[:, :D_MODEL] * jax.nn.sigmoid(z[:, D_MODEL:])
    o_ref[...] = h + _rms(mix, gpost_ref[...])


def _s5_params(log_dt, a_re, a_im, b_re, b_im, c_re, c_im):
    dt = jnp.exp(log_dt.astype(F32))[:, None]
    a_re = a_re.astype(F32)
    a_im = a_im.astype(F32)
    mag = jnp.exp(a_re * dt)
    lam_re = mag * jnp.cos(a_im * dt)
    lam_im = mag * jnp.sin(a_im * dt)
    den = a_re * a_re + a_im * a_im
    nr = lam_re - 1.0
    coef_re = (nr * a_re + lam_im * a_im) / den
    coef_im = (lam_im * a_re - nr * a_im) / den
    bb_re = coef_re[..., None] * b_re - coef_im[..., None] * b_im
    bb_im = coef_re[..., None] * b_im + coef_im[..., None] * b_re

    eye = jnp.eye(S5_TILE_GROUPS, dtype=F32)

    def in_proj(b):
        b = b.reshape(S5_KTILES, S5_TILE_GROUPS, SSM_STATE, SSM_GROUP)
        w = jnp.einsum('kgpc,gh->kgchp', b, eye)
        return w.reshape(S5_KTILES, V7X_MXU_DIM, S5_TILE_STATES)

    def out_proj(c):
        c = c.astype(F32).reshape(S5_KTILES, S5_TILE_GROUPS, SSM_GROUP, SSM_STATE)
        w = jnp.einsum('kgcp,gh->kgphc', c, eye)
        return w.reshape(S5_KTILES, S5_TILE_STATES, V7X_MXU_DIM)

    wb = jnp.concatenate([in_proj(bb_re), in_proj(bb_im)], axis=2).astype(BF16)
    wc = jnp.concatenate([out_proj(c_re), -out_proj(c_im)], axis=1).astype(BF16)

    def tile(v):
        return v.reshape(S5_KTILES, 1, S5_TILE_STATES)

    seg_mag = jnp.exp(float(S5_SEG) * a_re * dt)
    m_re = seg_mag * jnp.cos(float(S5_SEG) * a_im * dt)
    m_im = seg_mag * jnp.sin(float(S5_SEG) * a_im * dt)
    return wb, wc, tile(lam_re), tile(lam_im), tile(m_re), tile(m_im)


def _s5_layer(h, g_pre, g_post, params, d_skip, w_glu, b_glu):
    seq = h.shape[0]
    row_spec = pl.BlockSpec((S5_ROWS, D_MODEL), lambda i: (i, 0))
    args = (h, g_pre.reshape(1, D_MODEL), g_post.reshape(1, D_MODEL),
            *params, d_skip.reshape(1, D_MODEL).astype(F32), w_glu.astype(BF16),
            b_glu.reshape(1, 2 * D_MODEL).astype(F32))
    in_specs = [row_spec] + [_const_spec(a.shape) for a in args[1:]]
    return pl.pallas_call(
        _s5_kernel,
        name="s5_mixer",
        grid=(seq // S5_ROWS,),
        in_specs=in_specs,
        out_specs=row_spec,
        out_shape=jax.ShapeDtypeStruct((seq, D_MODEL), F32),
        scratch_shapes=[
            pltpu.VMEM((2, S5_ROWS, S5_TILE_STATES), F32),
            pltpu.VMEM((2, S5_ROWS, S5_TILE_STATES), F32),
            pltpu.VMEM((2, S5_ROWS, 2 * S5_TILE_STATES), BF16),
            pltpu.VMEM((S5_KTILES, 1, S5_TILE_STATES), F32),
            pltpu.VMEM((S5_KTILES, 1, S5_TILE_STATES), F32),
            pltpu.VMEM((S5_ROWS, D_MODEL), F32),
        ],
        compiler_params=pltpu.CompilerParams(
            dimension_semantics=("arbitrary",), vmem_limit_bytes=_vmem_limit(56 * 1024 * 1024)),
    )(*args)


def _ffn_body(h, gpre_ref, gpost_ref, win_ref, wout_ref, o_ref):
    xn = _rms(h, gpre_ref[...]).astype(BF16)
    hd = _dot(xn, win_ref[...])
    a = (jax.nn.silu(hd[:, :D_FF]) * hd[:, D_FF:]).astype(BF16)
    f = _dot(a, wout_ref[...])
    o_ref[...] = h + _rms(f, gpost_ref[...])


def _ffn_kernel(h_ref, gpre_ref, gpost_ref, win_ref, wout_ref, o_ref):
    _ffn_body(h_ref[...], gpre_ref, gpost_ref, win_ref, wout_ref, o_ref)


def _oproj_ffn_kernel(h_ref, att_ref, wo_ref, gmix_ref, gpre_ref, gpost_ref, win_ref, wout_ref, o_ref):
    h = h_ref[...] + _rms(_dot(att_ref[...], wo_ref[...]), gmix_ref[...])
    _ffn_body(h, gpre_ref, gpost_ref, win_ref, wout_ref, o_ref)


def _ffn_layer(h, g_pre, g_post, w_in, w_out, att=None, w_o=None, g_mix=None):
    seq = h.shape[0]
    row_spec = pl.BlockSpec((FFN_ROWS, D_MODEL), lambda i: (i, 0))
    tail = (g_pre.reshape(1, D_MODEL), g_post.reshape(1, D_MODEL), w_in.astype(BF16), w_out.astype(BF16))
    if att is None:
        kern, args, specs = _ffn_kernel, (h,) + tail, [row_spec]
    else:
        head = (att, w_o.astype(BF16), g_mix.reshape(1, D_MODEL))
        kern, args = _oproj_ffn_kernel, (h,) + head + tail
        specs = [row_spec, row_spec, _const_spec(head[1].shape), _const_spec(head[2].shape)]
    specs = specs + [_const_spec(a.shape) for a in tail]
    return pl.pallas_call(
        kern,
        name="ffn",
        grid=(seq // FFN_ROWS,),
        in_specs=specs,
        out_specs=row_spec,
        out_shape=jax.ShapeDtypeStruct((seq, D_MODEL), F32),
        compiler_params=pltpu.CompilerParams(
            dimension_semantics=("arbitrary",), vmem_limit_bytes=_vmem_limit(56 * 1024 * 1024)),
    )(*args)


def _q_of(h, gq_ref, wq_ref):
    u = _rms(h, gq_ref[...]).astype(BF16)
    return (_dot(u, wq_ref[...]) * (HEAD_DIM ** -0.5 * LOG2_E)).astype(BF16)


def _qkv_kernel(h_ref, gq_ref, wq_ref, gkv_ref, wkt_ref, wv_ref, q_ref, kt_ref, v_ref):
    h = h_ref[...]
    q_ref[...] = _q_of(h, gq_ref, wq_ref)
    hk = _rms(h, gkv_ref[...]).astype(BF16)
    kt_ref[...] = lax.dot_general(wkt_ref[...], hk, (((1,), (1,)), ((), ())),
                                  preferred_element_type=F32).astype(BF16)
    v_ref[...] = _dot(hk, wv_ref[...]).astype(BF16)


def _q_kernel(h_ref, gq_ref, wq_ref, q_ref):
    q_ref[...] = _q_of(h_ref[...], gq_ref, wq_ref)


def _qkv_proj(h, g_q, w_q, g_kv, w_k, w_v):
    seq = h.shape[0]
    row_spec = pl.BlockSpec((PROJ_ROWS, D_MODEL), lambda i: (i, 0))
    col_spec = pl.BlockSpec((D_MODEL, PROJ_ROWS), lambda i: (0, i))
    args = (h, g_q.reshape(1, D_MODEL), w_q.astype(BF16), g_kv.reshape(1, D_MODEL),
            w_k.T.astype(BF16), w_v.astype(BF16))
    return pl.pallas_call(
        _qkv_kernel,
        name="qkv_proj",
        grid=(seq // PROJ_ROWS,),
        in_specs=[row_spec] + [_const_spec(a.shape) for a in args[1:]],
        out_specs=[row_spec, col_spec, row_spec],
        out_shape=[jax.ShapeDtypeStruct((seq, D_MODEL), BF16),
                   jax.ShapeDtypeStruct((D_MODEL, seq), BF16),
                   jax.ShapeDtypeStruct((seq, D_MODEL), BF16)],
        compiler_params=pltpu.CompilerParams(
            dimension_semantics=("arbitrary",), vmem_limit_bytes=_vmem_limit(56 * 1024 * 1024)),
    )(*args)


def _q_proj(h, g_q, w_q):
    seq = h.shape[0]
    row_spec = pl.BlockSpec((PROJ_ROWS, D_MODEL), lambda i: (i, 0))
    args = (h, g_q.reshape(1, D_MODEL), w_q.astype(BF16))
    return pl.pallas_call(
        _q_kernel,
        name="q_proj",
        grid=(seq // PROJ_ROWS,),
        in_specs=[row_spec] + [_const_spec(a.shape) for a in args[1:]],
        out_specs=row_spec,
        out_shape=jax.ShapeDtypeStruct((seq, D_MODEL), BF16),
        compiler_params=pltpu.CompilerParams(
            dimension_semantics=("arbitrary",), vmem_limit_bytes=_vmem_limit(40 * 1024 * 1024)),
    )(*args)


def _attn_kernel(q_ref, kt_ref, v_ref, o_ref, qm_ref, acc_ref, lr_ref):
    tile = pl.program_id(1)
    lane = lax.broadcasted_iota(jnp.int32, (ATT_KEYS, V7X_LANES), 1)
    head0 = lane < HEAD_DIM
    srow = lax.broadcasted_iota(jnp.int32, (ATT_PAIR_ROWS, ATT_KEYS), 0)
    skey = lax.broadcasted_iota(jnp.int32, (ATT_PAIR_ROWS, ATT_KEYS), 1)
    causal = jnp.where(srow >= ATT_KEYS, srow - ATT_KEYS, srow) > skey
    trow = lax.broadcasted_iota(jnp.int32, (ATT_KEYS, 2 * ATT_KEYS), 0)
    tcol = lax.broadcasted_iota(jnp.int32, (ATT_KEYS, 2 * ATT_KEYS), 1)
    tri_ones = jnp.where((trow >= tcol) | (tcol >= ATT_KEYS), 1.0, 0.0).astype(BF16)

    for qb in range(ATT_QBLOCKS):
        q = q_ref[qb * ATT_KEYS:(qb + 1) * ATT_KEYS, :]
        zero = jnp.zeros_like(q)
        qm_ref[qb] = jnp.concatenate([jnp.where(head0, q, zero), jnp.where(head0, zero, q)], axis=0)

    half = ATT_KEYS // 2
    early = [slice(hh * ATT_KEYS, hh * ATT_KEYS + half) for hh in range(HEADS_PER_STEP)]
    late = [slice(hh * ATT_KEYS + half, (hh + 1) * ATT_KEYS) for hh in range(HEADS_PER_STEP)]

    def pick(ref, qb, parts):
        return ref[qb] if parts is None else jnp.concatenate([ref[qb, p, :] for p in parts], axis=0)

    def step(kv_blocks, keep, state, parts=None):
        nrows = ATT_PAIR_ROWS if parts is None else ATT_PAIR_ROWS // 2
        starts = [pl.multiple_of(kb * ATT_KEYS, ATT_KEYS) for kb in kv_blocks]
        zs = [_dot(pick(qm_ref, qb, parts), kt_ref[:, pl.ds(starts[qb], ATT_KEYS)]) for qb in range(ATT_QBLOCKS)]
        sps = []
        for qb in range(ATT_QBLOCKS):
            sp = jnp.maximum(zs[qb], 0.0) + jnp.log2(1.0 + jnp.exp2(_neg_abs(zs[qb])))
            sps.append(sp if keep is None else jnp.where(keep[qb], sp, 0.0))
        sums = _dot(jnp.concatenate(sps, axis=0).astype(BF16), tri_ones)
        out = []
        for qb in range(ATT_QBLOCKS):
            rows = slice(qb * nrows, (qb + 1) * nrows)
            logw = zs[qb] - sums[rows, :ATT_KEYS]
            total = sums[rows, ATT_KEYS:]
            if state is not None:
                acc, log_rest = state[qb]
                logw = logw + log_rest
            w = jnp.exp2(logw)
            if keep is not None:
                w = jnp.where(keep[qb], w, 0.0)
            pv = _dot(w.astype(BF16), v_ref[pl.ds(starts[qb], ATT_KEYS), :])
            out.append((pv, -total) if state is None else (acc + pv, log_rest - total))
        return out

    def load_state(parts=None):
        return [(pick(acc_ref, qb, parts), pick(lr_ref, qb, parts)) for qb in range(ATT_QBLOCKS)]

    def store_state(state):
        top = None
        for qb, (acc, log_rest) in enumerate(state):
            acc_ref[qb] = acc
            lr_ref[qb] = log_rest
            top = log_rest if top is None else jnp.maximum(top, log_rest)
        top_early = jnp.max(jnp.maximum(top[early[0]], top[early[1]]))
        top_late = jnp.max(jnp.maximum(top[late[0]], top[late[1]]))
        return top_early, top_late

    def store_early(state):
        top = None
        for qb, (acc, log_rest) in enumerate(state):
            for n, p in enumerate(early):
                acc_ref[qb, p, :] = acc[n * half:(n + 1) * half]
                lr_ref[qb, p, :] = log_rest[n * half:(n + 1) * half]
            top = log_rest if top is None else jnp.maximum(top, log_rest)
        return jnp.max(top)

    first_block = tile * ATT_QBLOCKS
    state = step([first_block + qb for qb in range(ATT_QBLOCKS)], [causal] * ATT_QBLOCKS, None)
    before = [first_block + qb - 1 for qb in range(ATT_QBLOCKS)]
    tops = store_state(step([jnp.maximum(kb, 0) for kb in before], [kb >= 0 for kb in before], state))

    def walk_full(carry):
        s = carry[0]
        return (s + 1,) + store_state(step([first_block + qb - s for qb in range(ATT_QBLOCKS)], None, load_state()))

    def walk_early(carry):
        s, _, top_late = carry
        state = step([first_block + qb - s for qb in range(ATT_QBLOCKS)], None, load_state(early), early)
        return s + 1, store_early(state), top_late

    def walk_ragged(carry):
        s = carry[0]
        blocks = [first_block + qb - s for qb in range(ATT_QBLOCKS)]
        state = step([jnp.maximum(kb, 0) for kb in blocks], [kb >= 0 for kb in blocks], load_state())
        return (s + 1,) + store_state(state)

    carry = (jnp.int32(2),) + tops
    carry = lax.while_loop(lambda c: (c[0] <= first_block) & (c[2] > ATT_LOG2_CUTOFF), walk_full, carry)
    carry = lax.while_loop(lambda c: (c[0] <= first_block) & (c[1] > ATT_LOG2_CUTOFF), walk_early, carry)
    lax.while_loop(lambda c: (c[0] < first_block + ATT_QBLOCKS) & (jnp.maximum(c[1], c[2]) > ATT_LOG2_CUTOFF),
                   walk_ragged, carry)

    for qb in range(ATT_QBLOCKS):
        acc = acc_ref[qb]
        out = jnp.where(head0, acc[:ATT_KEYS], acc[ATT_KEYS:])
        o_ref[qb * ATT_KEYS:(qb + 1) * ATT_KEYS, :] = out.astype(BF16)


def _attention(q, kt, v):
    seq = q.shape[0]
    assert HEADS_PER_STEP == 2
    state = pltpu.VMEM((ATT_QBLOCKS, ATT_PAIR_ROWS, ATT_KEYS), F32)
    stacked_q = pltpu.VMEM((ATT_QBLOCKS, ATT_PAIR_ROWS, V7X_LANES), BF16)
    return pl.pallas_call(
        _attn_kernel,
        name="stickbreaking_attention",
        grid=(N_HEADS // HEADS_PER_STEP, seq // ATT_ROWS),
        in_specs=[pl.BlockSpec((ATT_ROWS, V7X_LANES), lambda p, i: (i, p)),
                  pl.BlockSpec((V7X_LANES, seq), lambda p, i: (p, 0)),
                  pl.BlockSpec((seq, V7X_LANES), lambda p, i: (0, p))],
        out_specs=pl.BlockSpec((ATT_ROWS, V7X_LANES), lambda p, i: (i, p)),
        out_shape=jax.ShapeDtypeStruct((seq, D_MODEL), BF16),
        scratch_shapes=[stacked_q, state, state],
        compiler_params=pltpu.CompilerParams(
            dimension_semantics=("arbitrary", "arbitrary"), vmem_limit_bytes=_vmem_limit(40 * 1024 * 1024)),
    )(q, kt, v)


def kernel(x, norm_mix_pre, norm_mix_post, norm_ffn_pre, norm_ffn_post, w_ffn_in, w_ffn_out, s5_log_dt, s5_a_re, s5_a_im, s5_b_re, s5_b_im, s5_c_re, s5_c_im, s5_d, s5_w_glu, s5_b_glu, kv_norm, w_k, w_v, w_q, w_o):
    bsz, seq, _ = x.shape
    assert bsz == 1 and seq % S5_ROWS == 0 and seq % ATT_ROWS == 0 and seq % FFN_ROWS == 0
    blocks = seq // S5_ROWS
    h = x.reshape(blocks, V7X_SUBLANES, S5_SEG, D_MODEL).transpose(0, 2, 1, 3).reshape(seq, D_MODEL)
    s5_params = jax.vmap(_s5_params)(s5_log_dt, s5_a_re, s5_a_im, s5_b_re, s5_b_im, s5_c_re, s5_c_im)
    for layer in range(N_A_LAYERS):
        params = tuple(p[layer] for p in s5_params)
        h = _s5_layer(h, norm_mix_pre[layer], norm_mix_post[layer], params, s5_d[layer],
                      s5_w_glu[layer], s5_b_glu[layer])
        h = _ffn_layer(h, norm_ffn_pre[layer], norm_ffn_post[layer], w_ffn_in[layer], w_ffn_out[layer])
    h = h.reshape(blocks, S5_SEG, V7X_SUBLANES, D_MODEL).transpose(0, 2, 1, 3).reshape(seq, D_MODEL)

    kt = v = None
    for layer in range(N_A_LAYERS, DEPTH):
        b = layer - N_A_LAYERS
        if b == 0:
            q, kt, v = _qkv_proj(h, norm_mix_pre[layer], w_q[b], kv_norm, w_k, w_v)
        else:
            q = _q_proj(h, norm_mix_pre[layer], w_q[b])
        att = _attention(q, kt, v)
        h = _ffn_layer(h, norm_ffn_pre[layer], norm_ffn_post[layer], w_ffn_in[layer], w_ffn_out[layer],
                       att=att, w_o=w_o[b], g_mix=norm_mix_post[layer])
    return h.reshape(bsz, seq, D_MODEL)
```

```python
import functools
import math

import jax
import jax.numpy as jnp
from jax import lax
from jax.experimental import pallas as pl
from jax.experimental.pallas import tpu as pltpu

F32 = jnp.float32
BF16 = jnp.bfloat16

D_MODEL = 1024
DEPTH = 4
N_A_LAYERS = DEPTH // 2
SSM_GROUP = 16
SSM_GROUPS = D_MODEL // SSM_GROUP
SSM_STATE = 64
N_HEADS = 16
HEAD_DIM = D_MODEL // N_HEADS
D_FF = 2816
EPS = 1e-6

V7X_LANES = 128
V7X_SUBLANES = 8
BF16_ROWS = 2 * V7X_SUBLANES
V7X_MXU_DIM = 256
V7X_VMEM_BYTES = 64 * 1024 * 1024

S5_KTILES = D_MODEL // V7X_MXU_DIM
S5_TILE_GROUPS = V7X_MXU_DIM // SSM_GROUP
S5_TILE_STATES = S5_TILE_GROUPS * SSM_STATE
S5_SEG = 64
S5_ROWS = V7X_SUBLANES * S5_SEG
S5_COLS = 512

FFN_ROWS = 512
PROJ_ROWS = 512

ATT_KEYS = 128
ATT_QBLOCKS = 8
ATT_ROWS = ATT_QBLOCKS * ATT_KEYS
HEADS_PER_STEP = V7X_LANES // HEAD_DIM
ATT_PAIR_ROWS = HEADS_PER_STEP * ATT_KEYS
LOG2_E = math.log2(math.e)
ATT_LOG2_CUTOFF = -104.0 * LOG2_E


def _vmem_limit(nbytes):
    return int(min(nbytes, V7X_VMEM_BYTES - 6 * 1024 * 1024))


def _rms(x, g):
    return x * lax.rsqrt(jnp.mean(x * x, axis=-1, keepdims=True) + EPS) * g


def _dot(a, b):
    return jnp.dot(a, b, preferred_element_type=F32)


def _neg_abs(x):
    return -jnp.abs(x)


def _const_spec(shape):
    zeros = (0,) * len(shape)
    return pl.BlockSpec(shape, lambda *_: zeros, pipeline_mode=pl.Buffered(1))


def _s5_kernel(h_ref, gpre_ref, gpost_ref, wb_ref, wc_ref, lre_ref, lim_ref, mre_ref, mim_ref,
               d_ref, wglu_ref, bglu_ref, o_ref,
               xre_ref, xim_ref, xb_ref, cre_ref, cim_ref, y_ref):
    @pl.when(pl.program_id(0) == 0)
    def _():
        cre_ref[...] = jnp.zeros_like(cre_ref)
        cim_ref[...] = jnp.zeros_like(cim_ref)

    h = h_ref[...]
    u = _rms(h, gpre_ref[...])
    u_bf = u.astype(BF16)
    zeros = jnp.zeros((V7X_SUBLANES, S5_COLS), F32)
    row = lax.broadcasted_iota(jnp.int32, (V7X_SUBLANES, S5_COLS), 0)

    for kt in range(S5_KTILES):
        slot = kt % 2
        bu = _dot(u_bf[:, kt * V7X_MXU_DIM:(kt + 1) * V7X_MXU_DIM], wb_ref[kt])
        xre_ref[slot] = bu[:, :S5_TILE_STATES]
        xim_ref[slot] = bu[:, S5_TILE_STATES:]

        for cg in range(S5_TILE_STATES // S5_COLS):
            cs = slice(cg * S5_COLS, (cg + 1) * S5_COLS)
            lre = jnp.broadcast_to(lre_ref[kt, :, cs], (V7X_SUBLANES, S5_COLS))
            lim = jnp.broadcast_to(lim_ref[kt, :, cs], (V7X_SUBLANES, S5_COLS))

            er, ei = zeros, zeros
            for i in range(S5_SEG):
                rows = pl.ds(i * V7X_SUBLANES, V7X_SUBLANES)
                er, ei = (lre * er - lim * ei + xre_ref[slot, rows, cs],
                          lre * ei + lim * er + xim_ref[slot, rows, cs])
                xre_ref[slot, rows, cs] = er
                xim_ref[slot, rows, cs] = ei

            mre = mre_ref[kt, :, cs]
            mim = mim_ref[kt, :, cs]
            cur_r = cre_ref[kt, :, cs]
            cur_i = cim_ref[kt, :, cs]
            xin_r = zeros
            xin_i = zeros
            for r in range(V7X_SUBLANES):
                xin_r = jnp.where(row == r, cur_r, xin_r)
                xin_i = jnp.where(row == r, cur_i, xin_i)
                nxt_r = mre * cur_r - mim * cur_i + er[r:r + 1, :]
                nxt_i = mre * cur_i + mim * cur_r + ei[r:r + 1, :]
                cur_r, cur_i = nxt_r, nxt_i
            cre_ref[kt, :, cs] = cur_r
            cim_ref[kt, :, cs] = cur_i

            cs_im = slice(S5_TILE_STATES + cg * S5_COLS, S5_TILE_STATES + (cg + 1) * S5_COLS)

            cr, ci = xin_r, xin_i
            for j in range(S5_ROWS // BF16_ROWS):
                full_r, full_i = [], []
                for half in range(BF16_ROWS // V7X_SUBLANES):
                    cr, ci = lre * cr - lim * ci, lre * ci + lim * cr
                    rows = pl.ds(j * BF16_ROWS + half * V7X_SUBLANES, V7X_SUBLANES)
                    full_r.append(xre_ref[slot, rows, cs] + cr)
                    full_i.append(xim_ref[slot, rows, cs] + ci)
                packed = pl.ds(j * BF16_ROWS, BF16_ROWS)
                xb_ref[slot, packed, cs] = jnp.concatenate(full_r, axis=0).astype(BF16)
                xb_ref[slot, packed, cs_im] = jnp.concatenate(full_i, axis=0).astype(BF16)

        y_ref[:, kt * V7X_MXU_DIM:(kt + 1) * V7X_MXU_DIM] = _dot(xb_ref[slot], wc_ref[kt])

    y = y_ref[...] + d_ref[...] * u
    a = jax.nn.gelu(y).astype(BF16)
    z = _dot(a, wglu_ref[...]) + bglu_ref[...]
    mix = z[:, :D_MODEL] * jax.nn.sigmoid(z[:, D_MODEL:])
    o_ref[...] = h + _rms(mix, gpost_ref[...])


def _s5_params(log_dt, a_re, a_im, b_re, b_im, c_re, c_im):
    dt = jnp.exp(log_dt.astype(F32))[:, None]
    a_re = a_re.astype(F32)
    a_im = a_im.astype(F32)
    mag = jnp.exp(a_re * dt)
    lam_re = mag * jnp.cos(a_im * dt)
    lam_im = mag * jnp.sin(a_im * dt)
    den = a_re * a_re + a_im * a_im
    nr = lam_re - 1.0
    coef_re = (nr * a_re + lam_im * a_im) / den
    coef_im = (lam_im * a_re - nr * a_im) / den
    bb_re = coef_re[..., None] * b_re - coef_im[..., None] * b_im
    bb_im = coef_re[..., None] * b_im + coef_im[..., None] * b_re

    eye = jnp.eye(S5_TILE_GROUPS, dtype=F32)

    def in_proj(b):
        b = b.reshape(S5_KTILES, S5_TILE_GROUPS, SSM_STATE, SSM_GROUP)
        w = jnp.einsum('kgpc,gh->kgchp', b, eye)
        return w.reshape(S5_KTILES, V7X_MXU_DIM, S5_TILE_STATES)

    def out_proj(c):
        c = c.astype(F32).reshape(S5_KTILES, S5_TILE_GROUPS, SSM_GROUP, SSM_STATE)
        w = jnp.einsum('kgcp,gh->kgphc', c, eye)
        return w.reshape(S5_KTILES, S5_TILE_STATES, V7X_MXU_DIM)

    wb = jnp.concatenate([in_proj(bb_re), in_proj(bb_im)], axis=2).astype(BF16)
    wc = jnp.concatenate([out_proj(c_re), -out_proj(c_im)], axis=1).astype(BF16)

    def tile(v):
        return v.reshape(S5_KTILES, 1, S5_TILE_STATES)

    seg_mag = jnp.exp(float(S5_SEG) * a_re * dt)
    m_re = seg_mag * jnp.cos(float(S5_SEG) * a_im * dt)
    m_im = seg_mag * jnp.sin(float(S5_SEG) * a_im * dt)
    return wb, wc, tile(lam_re), tile(lam_im), tile(m_re), tile(m_im)


def _s5_layer(h, g_pre, g_post, params, d_skip, w_glu, b_glu):
    seq = h.shape[0]
    row_spec = pl.BlockSpec((S5_ROWS, D_MODEL), lambda i: (i, 0))
    args = (h, g_pre.reshape(1, D_MODEL), g_post.reshape(1, D_MODEL),
            *params, d_skip.reshape(1, D_MODEL).astype(F32), w_glu.astype(BF16),
            b_glu.reshape(1, 2 * D_MODEL).astype(F32))
    in_specs = [row_spec] + [_const_spec(a.shape) for a in args[1:]]
    return pl.pallas_call(
        _s5_kernel,
        name="s5_mixer",
        grid=(seq // S5_ROWS,),
        in_specs=in_specs,
        out_specs=row_spec,
        out_shape=jax.ShapeDtypeStruct((seq, D_MODEL), F32),
        scratch_shapes=[
            pltpu.VMEM((2, S5_ROWS, S5_TILE_STATES), F32),
            pltpu.VMEM((2, S5_ROWS, S5_TILE_STATES), F32),
            pltpu.VMEM((2, S5_ROWS, 2 * S5_TILE_STATES), BF16),
            pltpu.VMEM((S5_KTILES, 1, S5_TILE_STATES), F32),
            pltpu.VMEM((S5_KTILES, 1, S5_TILE_STATES), F32),
            pltpu.VMEM((S5_ROWS, D_MODEL), F32),
        ],
        compiler_params=pltpu.CompilerParams(
            dimension_semantics=("arbitrary",), vmem_limit_bytes=_vmem_limit(56 * 1024 * 1024)),
    )(*args)


def _ffn_kernel(*refs, has_att, has_next_q):
    refs = list(refs)
    h = refs.pop(0)[...]
    if has_att:
        att_ref, wo_ref, gmix_ref = refs[:3]
        del refs[:3]
        h = h + _rms(_dot(att_ref[...], wo_ref[...]), gmix_ref[...])
    gpre_ref, gpost_ref, win_ref, wout_ref = refs[:4]
    del refs[:4]
    if has_next_q:
        gq_ref, wq_ref = refs[:2]
        del refs[:2]
    o_ref = refs.pop(0)
    xn = _rms(h, gpre_ref[...]).astype(BF16)
    hd = _dot(xn, win_ref[...])
    a = (jax.nn.silu(hd[:, :D_FF]) * hd[:, D_FF:]).astype(BF16)
    out = h + _rms(_dot(a, wout_ref[...]), gpost_ref[...])
    o_ref[...] = out
    if has_next_q:
        q_ref = refs.pop(0)
        q_ref[...] = _q_of(out, gq_ref, wq_ref)


def _ffn_layer(h, g_pre, g_post, w_in, w_out, att=None, w_o=None, g_mix=None, next_q=None):
    seq = h.shape[0]
    row_spec = pl.BlockSpec((FFN_ROWS, D_MODEL), lambda i: (i, 0))
    args, specs = [h], [row_spec]
    if att is not None:
        args += [att, w_o.astype(BF16), g_mix.reshape(1, D_MODEL)]
        specs += [row_spec, _const_spec((D_MODEL, D_MODEL)), _const_spec((1, D_MODEL))]
    consts = [g_pre.reshape(1, D_MODEL), g_post.reshape(1, D_MODEL), w_in.astype(BF16), w_out.astype(BF16)]
    if next_q is not None:
        consts += [next_q[0].reshape(1, D_MODEL), next_q[1].astype(BF16)]
    args += consts
    specs += [_const_spec(a.shape) for a in consts]
    out_specs, out_shape = [row_spec], [jax.ShapeDtypeStruct((seq, D_MODEL), F32)]
    if next_q is not None:
        out_specs.append(row_spec)
        out_shape.append(jax.ShapeDtypeStruct((seq, D_MODEL), BF16))
    outs = pl.pallas_call(
        functools.partial(_ffn_kernel, has_att=att is not None, has_next_q=next_q is not None),
        name="ffn",
        grid=(seq // FFN_ROWS,),
        in_specs=specs,
        out_specs=out_specs,
        out_shape=out_shape,
        compiler_params=pltpu.CompilerParams(
            dimension_semantics=("arbitrary",), vmem_limit_bytes=_vmem_limit(56 * 1024 * 1024)),
    )(*args)
    return outs[0] if next_q is None else tuple(outs)


def _q_of(h, gq_ref, wq_ref):
    u = _rms(h, gq_ref[...]).astype(BF16)
    return (_dot(u, wq_ref[...]) * (HEAD_DIM ** -0.5 * LOG2_E)).astype(BF16)


def _qkv_kernel(h_ref, gq_ref, wq_ref, gkv_ref, wkt_ref, wv_ref, q_ref, kt_ref, v_ref):
    h = h_ref[...]
    q_ref[...] = _q_of(h, gq_ref, wq_ref)
    hk = _rms(h, gkv_ref[...]).astype(BF16)
    kt_ref[...] = lax.dot_general(wkt_ref[...], hk, (((1,), (1,)), ((), ())),
                                  preferred_element_type=F32).astype(BF16)
    v_ref[...] = _dot(hk, wv_ref[...]).astype(BF16)


def _qkv_proj(h, g_q, w_q, g_kv, w_k, w_v):
    seq = h.shape[0]
    row_spec = pl.BlockSpec((PROJ_ROWS, D_MODEL), lambda i: (i, 0))
    col_spec = pl.BlockSpec((D_MODEL, PROJ_ROWS), lambda i: (0, i))
    args = (h, g_q.reshape(1, D_MODEL), w_q.astype(BF16), g_kv.reshape(1, D_MODEL),
            w_k.T.astype(BF16), w_v.astype(BF16))
    return pl.pallas_call(
        _qkv_kernel,
        name="qkv_proj",
        grid=(seq // PROJ_ROWS,),
        in_specs=[row_spec] + [_const_spec(a.shape) for a in args[1:]],
        out_specs=[row_spec, col_spec, row_spec],
        out_shape=[jax.ShapeDtypeStruct((seq, D_MODEL), BF16),
                   jax.ShapeDtypeStruct((D_MODEL, seq), BF16),
                   jax.ShapeDtypeStruct((seq, D_MODEL), BF16)],
        compiler_params=pltpu.CompilerParams(
            dimension_semantics=("arbitrary",), vmem_limit_bytes=_vmem_limit(40 * 1024 * 1024)),
    )(*args)


def _attn_kernel(q_ref, kt_ref, v_ref, o_ref, qm_ref, acc_ref, lr_ref):
    tile = pl.program_id(1)
    lane = lax.broadcasted_iota(jnp.int32, (ATT_KEYS, V7X_LANES), 1)
    head0 = lane < HEAD_DIM
    srow = lax.broadcasted_iota(jnp.int32, (ATT_PAIR_ROWS, ATT_KEYS), 0)
    skey = lax.broadcasted_iota(jnp.int32, (ATT_PAIR_ROWS, ATT_KEYS), 1)
    causal = jnp.where(srow >= ATT_KEYS, srow - ATT_KEYS, srow) > skey
    trow = lax.broadcasted_iota(jnp.int32, (ATT_KEYS, 2 * ATT_KEYS), 0)
    tcol = lax.broadcasted_iota(jnp.int32, (ATT_KEYS, 2 * ATT_KEYS), 1)
    tri_ones = jnp.where((trow >= tcol) | (tcol >= ATT_KEYS), 1.0, 0.0).astype(BF16)

    for qb in range(ATT_QBLOCKS):
        q = q_ref[qb * ATT_KEYS:(qb + 1) * ATT_KEYS, :]
        zero = jnp.zeros_like(q)
        qm_ref[qb] = jnp.concatenate([jnp.where(head0, q, zero), jnp.where(head0, zero, q)], axis=0)

    half = ATT_KEYS // 2
    early = [slice(hh * ATT_KEYS, hh * ATT_KEYS + half) for hh in range(HEADS_PER_STEP)]
    late = [slice(hh * ATT_KEYS + half, (hh + 1) * ATT_KEYS) for hh in range(HEADS_PER_STEP)]

    def pick(ref, qb, parts):
        return ref[qb] if parts is None else jnp.concatenate([ref[qb, p, :] for p in parts], axis=0)

    def step(kv_blocks, keep, state, parts=None):
        nrows = ATT_PAIR_ROWS if parts is None else ATT_PAIR_ROWS // 2
        starts = [pl.multiple_of(kb * ATT_KEYS, ATT_KEYS) for kb in kv_blocks]
        zs = [_dot(pick(qm_ref, qb, parts), kt_ref[:, pl.ds(starts[qb], ATT_KEYS)]) for qb in range(ATT_QBLOCKS)]
        sps = []
        for qb in range(ATT_QBLOCKS):
            sp = jnp.maximum(zs[qb], 0.0) + jnp.log2(1.0 + jnp.exp2(_neg_abs(zs[qb])))
            sps.append(sp if keep is None else jnp.where(keep[qb], sp, 0.0))
        sums = _dot(jnp.concatenate(sps, axis=0).astype(BF16), tri_ones)
        out = []
        for qb in range(ATT_QBLOCKS):
            rows = slice(qb * nrows, (qb + 1) * nrows)
            logw = zs[qb] - sums[rows, :ATT_KEYS]
            total = sums[rows, ATT_KEYS:]
            if state is not None:
                acc, log_rest = state[qb]
                logw = logw + log_rest
            w = jnp.exp2(logw)
            if keep is not None:
                w = jnp.where(keep[qb], w, 0.0)
            pv = _dot(w.astype(BF16), v_ref[pl.ds(starts[qb], ATT_KEYS), :])
            out.append((pv, -total) if state is None else (acc + pv, log_rest - total))
        return out

    def load_state(parts=None):
        return [(pick(acc_ref, qb, parts), pick(lr_ref, qb, parts)) for qb in range(ATT_QBLOCKS)]

    def store_state(state):
        top = None
        for qb, (acc, log_rest) in enumerate(state):
            acc_ref[qb] = acc
            lr_ref[qb] = log_rest
            top = log_rest if top is None else jnp.maximum(top, log_rest)
        top_early = jnp.max(jnp.maximum(top[early[0]], top[early[1]]))
        top_late = jnp.max(jnp.maximum(top[late[0]], top[late[1]]))
        return top_early, top_late

    def store_early(state):
        top = None
        for qb, (acc, log_rest) in enumerate(state):
            for n, p in enumerate(early):
                acc_ref[qb, p, :] = acc[n * half:(n + 1) * half]
                lr_ref[qb, p, :] = log_rest[n * half:(n + 1) * half]
            top = log_rest if top is None else jnp.maximum(top, log_rest)
        return jnp.max(top)

    first_block = tile * ATT_QBLOCKS
    state = step([first_block + qb for qb in range(ATT_QBLOCKS)], [causal] * ATT_QBLOCKS, None)
    before = [first_block + qb - 1 for qb in range(ATT_QBLOCKS)]
    tops = store_state(step([jnp.maximum(kb, 0) for kb in before], [kb >= 0 for kb in before], state))

    def walk_full(carry):
        s = carry[0]
        return (s + 1,) + store_state(step([first_block + qb - s for qb in range(ATT_QBLOCKS)], None, load_state()))

    def walk_early(carry):
        s, _, top_late = carry
        state = step([first_block + qb - s for qb in range(ATT_QBLOCKS)], None, load_state(early), early)
        return s + 1, store_early(state), top_late

    def walk_ragged(carry):
        s = carry[0]
        blocks = [first_block + qb - s for qb in range(ATT_QBLOCKS)]
        state = step([jnp.maximum(kb, 0) for kb in blocks], [kb >= 0 for kb in blocks], load_state())
        return (s + 1,) + store_state(state)

    carry = (jnp.int32(2),) + tops
    carry = lax.while_loop(lambda c: (c[0] <= first_block) & (c[2] > ATT_LOG2_CUTOFF), walk_full, carry)
    carry = lax.while_loop(lambda c: (c[0] <= first_block) & (c[1] > ATT_LOG2_CUTOFF), walk_early, carry)
    lax.while_loop(lambda c: (c[0] < first_block + ATT_QBLOCKS) & (jnp.maximum(c[1], c[2]) > ATT_LOG2_CUTOFF),
                   walk_ragged, carry)

    for qb in range(ATT_QBLOCKS):
        acc = acc_ref[qb]
        out = jnp.where(head0, acc[:ATT_KEYS], acc[ATT_KEYS:])
        o_ref[qb * ATT_KEYS:(qb + 1) * ATT_KEYS, :] = out.astype(BF16)


def _attention(q, kt, v):
    seq = q.shape[0]
    assert HEADS_PER_STEP == 2
    state = pltpu.VMEM((ATT_QBLOCKS, ATT_PAIR_ROWS, ATT_KEYS), F32)
    stacked_q = pltpu.VMEM((ATT_QBLOCKS, ATT_PAIR_ROWS, V7X_LANES), BF16)
    return pl.pallas_call(
        _attn_kernel,
        name="stickbreaking_attention",
        grid=(N_HEADS // HEADS_PER_STEP, seq // ATT_ROWS),
        in_specs=[pl.BlockSpec((ATT_ROWS, V7X_LANES), lambda p, i: (i, p)),
                  pl.BlockSpec((V7X_LANES, seq), lambda p, i: (p, 0)),
                  pl.BlockSpec((seq, V7X_LANES), lambda p, i: (0, p))],
        out_specs=pl.BlockSpec((ATT_ROWS, V7X_LANES), lambda p, i: (i, p)),
        out_shape=jax.ShapeDtypeStruct((seq, D_MODEL), BF16),
        scratch_shapes=[stacked_q, state, state],
        compiler_params=pltpu.CompilerParams(
            dimension_semantics=("arbitrary", "arbitrary"), vmem_limit_bytes=_vmem_limit(40 * 1024 * 1024)),
    )(q, kt, v)


def kernel(x, norm_mix_pre, norm_mix_post, norm_ffn_pre, norm_ffn_post, w_ffn_in, w_ffn_out, s5_log_dt, s5_a_re, s5_a_im, s5_b_re, s5_b_im, s5_c_re, s5_c_im, s5_d, s5_w_glu, s5_b_glu, kv_norm, w_k, w_v, w_q, w_o):
    bsz, seq, _ = x.shape
    assert bsz == 1 and seq % S5_ROWS == 0 and seq % ATT_ROWS == 0 and seq % FFN_ROWS == 0
    blocks = seq // S5_ROWS
    h = x.reshape(blocks, V7X_SUBLANES, S5_SEG, D_MODEL).transpose(0, 2, 1, 3).reshape(seq, D_MODEL)
    s5_params = jax.vmap(_s5_params)(s5_log_dt, s5_a_re, s5_a_im, s5_b_re, s5_b_im, s5_c_re, s5_c_im)
    for layer in range(N_A_LAYERS):
        params = tuple(p[layer] for p in s5_params)
        h = _s5_layer(h, norm_mix_pre[layer], norm_mix_post[layer], params, s5_d[layer],
                      s5_w_glu[layer], s5_b_glu[layer])
        h = _ffn_layer(h, norm_ffn_pre[layer], norm_ffn_post[layer], w_ffn_in[layer], w_ffn_out[layer])
    h = h.reshape(blocks, S5_SEG, V7X_SUBLANES, D_MODEL).transpose(0, 2, 1, 3).reshape(seq, D_MODEL)

    q, kt, v = _qkv_proj(h, norm_mix_pre[N_A_LAYERS], w_q[0], kv_norm, w_k, w_v)
    for layer in range(N_A_LAYERS, DEPTH):
        b = layer - N_A_LAYERS
        att = _attention(q, kt, v)
        next_q = (norm_mix_pre[layer + 1], w_q[b + 1]) if layer + 1 < DEPTH else None
        out = _ffn_layer(h, norm_ffn_pre[layer], norm_ffn_post[layer], w_ffn_in[layer], w_ffn_out[layer],
                         att=att, w_o=w_o[b], g_mix=norm_mix_post[layer], next_q=next_q)
        h, q = out if next_q is not None else (out, None)
    return h.reshape(bsz, seq, D_MODEL)
```

```python
import functools
import math

import jax
import jax.numpy as jnp
from jax import lax
from jax.experimental import pallas as pl
from jax.experimental.pallas import tpu as pltpu

F32 = jnp.float32
BF16 = jnp.bfloat16

D_MODEL = 1024
DEPTH = 4
N_A_LAYERS = DEPTH // 2
SSM_GROUP = 16
SSM_GROUPS = D_MODEL // SSM_GROUP
SSM_STATE = 64
N_HEADS = 16
HEAD_DIM = D_MODEL // N_HEADS
D_FF = 2816
EPS = 1e-6

V7X_LANES = 128
V7X_SUBLANES = 8
BF16_ROWS = 2 * V7X_SUBLANES
V7X_MXU_DIM = 256
V7X_VMEM_BYTES = 64 * 1024 * 1024

S5_KTILES = D_MODEL // V7X_MXU_DIM
S5_TILE_GROUPS = V7X_MXU_DIM // SSM_GROUP
S5_TILE_STATES = S5_TILE_GROUPS * SSM_STATE
S5_SEG = 64
S5_ROWS = V7X_SUBLANES * S5_SEG
S5_COLS = 512

FFN_ROWS = 512
PROJ_ROWS = 512

ATT_KEYS = 128
ATT_QBLOCKS = 8
ATT_ROWS = ATT_QBLOCKS * ATT_KEYS
HEADS_PER_STEP = V7X_LANES // HEAD_DIM
ATT_PAIR_ROWS = HEADS_PER_STEP * ATT_KEYS
LOG2_E = math.log2(math.e)
ATT_LOG2_CUTOFF = -104.0 * LOG2_E
ATT_MASKED_LOGIT = -1e30


def _vmem_limit(nbytes):
    return int(min(nbytes, V7X_VMEM_BYTES - 6 * 1024 * 1024))


def _rms(x, g):
    return x * lax.rsqrt(jnp.mean(x * x, axis=-1, keepdims=True) + EPS) * g


def _dot(a, b):
    return jnp.dot(a, b, preferred_element_type=F32)


def _neg_abs(x):
    return -jnp.abs(x)


def _const_spec(shape):
    zeros = (0,) * len(shape)
    return pl.BlockSpec(shape, lambda *_: zeros, pipeline_mode=pl.Buffered(1))


def _s5_kernel(h_ref, gpre_ref, gpost_ref, wb_ref, wc_ref, lre_ref, lim_ref, mre_ref, mim_ref,
               d_ref, wglu_ref, bglu_ref, o_ref,
               xre_ref, xim_ref, xb_ref, cre_ref, cim_ref, y_ref):
    @pl.when(pl.program_id(0) == 0)
    def _():
        cre_ref[...] = jnp.zeros_like(cre_ref)
        cim_ref[...] = jnp.zeros_like(cim_ref)

    h = h_ref[...]
    u = _rms(h, gpre_ref[...])
    u_bf = u.astype(BF16)
    zeros = jnp.zeros((V7X_SUBLANES, S5_COLS), F32)
    row = lax.broadcasted_iota(jnp.int32, (V7X_SUBLANES, S5_COLS), 0)

    for kt in range(S5_KTILES):
        slot = kt % 2
        bu = _dot(u_bf[:, kt * V7X_MXU_DIM:(kt + 1) * V7X_MXU_DIM], wb_ref[kt])
        xre_ref[slot] = bu[:, :S5_TILE_STATES]
        xim_ref[slot] = bu[:, S5_TILE_STATES:]

        for cg in range(S5_TILE_STATES // S5_COLS):
            cs = slice(cg * S5_COLS, (cg + 1) * S5_COLS)
            lre = jnp.broadcast_to(lre_ref[kt, :, cs], (V7X_SUBLANES, S5_COLS))
            lim = jnp.broadcast_to(lim_ref[kt, :, cs], (V7X_SUBLANES, S5_COLS))

            er, ei = zeros, zeros
            for i in range(S5_SEG):
                rows = pl.ds(i * V7X_SUBLANES, V7X_SUBLANES)
                er, ei = (lre * er - lim * ei + xre_ref[slot, rows, cs],
                          lre * ei + lim * er + xim_ref[slot, rows, cs])
                xre_ref[slot, rows, cs] = er
                xim_ref[slot, rows, cs] = ei

            mre = mre_ref[kt, :, cs]
            mim = mim_ref[kt, :, cs]
            cur_r = cre_ref[kt, :, cs]
            cur_i = cim_ref[kt, :, cs]
            xin_r = zeros
            xin_i = zeros
            for r in range(V7X_SUBLANES):
                xin_r = jnp.where(row == r, cur_r, xin_r)
                xin_i = jnp.where(row == r, cur_i, xin_i)
                nxt_r = mre * cur_r - mim * cur_i + er[r:r + 1, :]
                nxt_i = mre * cur_i + mim * cur_r + ei[r:r + 1, :]
                cur_r, cur_i = nxt_r, nxt_i
            cre_ref[kt, :, cs] = cur_r
            cim_ref[kt, :, cs] = cur_i

            cs_im = slice(S5_TILE_STATES + cg * S5_COLS, S5_TILE_STATES + (cg + 1) * S5_COLS)

            cr, ci = xin_r, xin_i
            for j in range(S5_ROWS // BF16_ROWS):
                full_r, full_i = [], []
                for half in range(BF16_ROWS // V7X_SUBLANES):
                    cr, ci = lre * cr - lim * ci, lre * ci + lim * cr
                    rows = pl.ds(j * BF16_ROWS + half * V7X_SUBLANES, V7X_SUBLANES)
                    full_r.append(xre_ref[slot, rows, cs] + cr)
                    full_i.append(xim_ref[slot, rows, cs] + ci)
                packed = pl.ds(j * BF16_ROWS, BF16_ROWS)
                xb_ref[slot, packed, cs] = jnp.concatenate(full_r, axis=0).astype(BF16)
                xb_ref[slot, packed, cs_im] = jnp.concatenate(full_i, axis=0).astype(BF16)

        y_ref[:, kt * V7X_MXU_DIM:(kt + 1) * V7X_MXU_DIM] = _dot(xb_ref[slot], wc_ref[kt])

    y = y_ref[...] + d_ref[...] * u
    a = jax.nn.gelu(y).astype(BF16)
    z = _dot(a, wglu_ref[...]) + bglu_ref[...]
    mix = z[:, :D_MODEL] * jax.nn.sigmoid(z[:, D_MODEL:])
    o_ref[...] = h + _rms(mix, gpost_ref[...])


def _s5_params(log_dt, a_re, a_im, b_re, b_im, c_re, c_im):
    dt = jnp.exp(log_dt.astype(F32))[:, None]
    a_re = a_re.astype(F32)
    a_im = a_im.astype(F32)
    mag = jnp.exp(a_re * dt)
    lam_re = mag * jnp.cos(a_im * dt)
    lam_im = mag * jnp.sin(a_im * dt)
    den = a_re * a_re + a_im * a_im
    nr = lam_re - 1.0
    coef_re = (nr * a_re + lam_im * a_im) / den
    coef_im = (lam_im * a_re - nr * a_im) / den
    bb_re = coef_re[..., None] * b_re - coef_im[..., None] * b_im
    bb_im = coef_re[..., None] * b_im + coef_im[..., None] * b_re

    eye = jnp.eye(S5_TILE_GROUPS, dtype=F32)

    def in_proj(b):
        b = b.reshape(S5_KTILES, S5_TILE_GROUPS, SSM_STATE, SSM_GROUP)
        w = jnp.einsum('kgpc,gh->kgchp', b, eye)
        return w.reshape(S5_KTILES, V7X_MXU_DIM, S5_TILE_STATES)

    def out_proj(c):
        c = c.astype(F32).reshape(S5_KTILES, S5_TILE_GROUPS, SSM_GROUP, SSM_STATE)
        w = jnp.einsum('kgcp,gh->kgphc', c, eye)
        return w.reshape(S5_KTILES, S5_TILE_STATES, V7X_MXU_DIM)

    wb = jnp.concatenate([in_proj(bb_re), in_proj(bb_im)], axis=2).astype(BF16)
    wc = jnp.concatenate([out_proj(c_re), -out_proj(c_im)], axis=1).astype(BF16)

    def tile(v):
        return v.reshape(S5_KTILES, 1, S5_TILE_STATES)

    seg_mag = jnp.exp(float(S5_SEG) * a_re * dt)
    m_re = seg_mag * jnp.cos(float(S5_SEG) * a_im * dt)
    m_im = seg_mag * jnp.sin(float(S5_SEG) * a_im * dt)
    return wb, wc, tile(lam_re), tile(lam_im), tile(m_re), tile(m_im)


def _s5_layer(h, g_pre, g_post, params, d_skip, w_glu, b_glu):
    seq = h.shape[0]
    row_spec = pl.BlockSpec((S5_ROWS, D_MODEL), lambda i: (i, 0))
    args = (h, g_pre.reshape(1, D_MODEL), g_post.reshape(1, D_MODEL),
            *params, d_skip.reshape(1, D_MODEL).astype(F32), w_glu.astype(BF16),
            b_glu.reshape(1, 2 * D_MODEL).astype(F32))
    in_specs = [row_spec] + [_const_spec(a.shape) for a in args[1:]]
    return pl.pallas_call(
        _s5_kernel,
        name="s5_mixer",
        grid=(seq // S5_ROWS,),
        in_specs=in_specs,
        out_specs=row_spec,
        out_shape=jax.ShapeDtypeStruct((seq, D_MODEL), F32),
        scratch_shapes=[
            pltpu.VMEM((2, S5_ROWS, S5_TILE_STATES), F32),
            pltpu.VMEM((2, S5_ROWS, S5_TILE_STATES), F32),
            pltpu.VMEM((2, S5_ROWS, 2 * S5_TILE_STATES), BF16),
            pltpu.VMEM((S5_KTILES, 1, S5_TILE_STATES), F32),
            pltpu.VMEM((S5_KTILES, 1, S5_TILE_STATES), F32),
            pltpu.VMEM((S5_ROWS, D_MODEL), F32),
        ],
        compiler_params=pltpu.CompilerParams(
            dimension_semantics=("arbitrary",), vmem_limit_bytes=_vmem_limit(56 * 1024 * 1024)),
    )(*args)


def _ffn_kernel(*refs, has_att, has_next_q):
    refs = list(refs)
    h = refs.pop(0)[...]
    if has_att:
        att_ref, wo_ref, gmix_ref = refs[:3]
        del refs[:3]
        h = h + _rms(_dot(att_ref[...], wo_ref[...]), gmix_ref[...])
    gpre_ref, gpost_ref, win_ref, wout_ref = refs[:4]
    del refs[:4]
    if has_next_q:
        gq_ref, wq_ref = refs[:2]
        del refs[:2]
    o_ref = refs.pop(0)
    xn = _rms(h, gpre_ref[...]).astype(BF16)
    hd = _dot(xn, win_ref[...])
    a = (jax.nn.silu(hd[:, :D_FF]) * hd[:, D_FF:]).astype(BF16)
    out = h + _rms(_dot(a, wout_ref[...]), gpost_ref[...])
    o_ref[...] = out
    if has_next_q:
        q_ref = refs.pop(0)
        q_ref[...] = _q_of(out, gq_ref, wq_ref)


def _ffn_layer(h, g_pre, g_post, w_in, w_out, att=None, w_o=None, g_mix=None, next_q=None):
    seq = h.shape[0]
    row_spec = pl.BlockSpec((FFN_ROWS, D_MODEL), lambda i: (i, 0))
    args, specs = [h], [row_spec]
    if att is not None:
        args += [att, w_o.astype(BF16), g_mix.reshape(1, D_MODEL)]
        specs += [row_spec, _const_spec((D_MODEL, D_MODEL)), _const_spec((1, D_MODEL))]
    consts = [g_pre.reshape(1, D_MODEL), g_post.reshape(1, D_MODEL), w_in.astype(BF16), w_out.astype(BF16)]
    if next_q is not None:
        consts += [next_q[0].reshape(1, D_MODEL), next_q[1].astype(BF16)]
    args += consts
    specs += [_const_spec(a.shape) for a in consts]
    out_specs, out_shape = [row_spec], [jax.ShapeDtypeStruct((seq, D_MODEL), F32)]
    if next_q is not None:
        out_specs.append(row_spec)
        out_shape.append(jax.ShapeDtypeStruct((seq, D_MODEL), BF16))
    outs = pl.pallas_call(
        functools.partial(_ffn_kernel, has_att=att is not None, has_next_q=next_q is not None),
        name="ffn",
        grid=(seq // FFN_ROWS,),
        in_specs=specs,
        out_specs=out_specs,
        out_shape=out_shape,
        compiler_params=pltpu.CompilerParams(
            dimension_semantics=("arbitrary",), vmem_limit_bytes=_vmem_limit(56 * 1024 * 1024)),
    )(*args)
    return outs[0] if next_q is None else tuple(outs)


def _q_of(h, gq_ref, wq_ref):
    u = _rms(h, gq_ref[...]).astype(BF16)
    return (_dot(u, wq_ref[...]) * (HEAD_DIM ** -0.5 * LOG2_E)).astype(BF16)


def _qkv_kernel(h_ref, gq_ref, wq_ref, gkv_ref, wkt_ref, wv_ref, q_ref, kt_ref, v_ref):
    h = h_ref[...]
    q_ref[...] = _q_of(h, gq_ref, wq_ref)
    hk = _rms(h, gkv_ref[...]).astype(BF16)
    kt_ref[...] = lax.dot_general(wkt_ref[...], hk, (((1,), (1,)), ((), ())),
                                  preferred_element_type=F32).astype(BF16)
    v_ref[...] = _dot(hk, wv_ref[...]).astype(BF16)


def _qkv_proj(h, g_q, w_q, g_kv, w_k, w_v):
    seq = h.shape[0]
    row_spec = pl.BlockSpec((PROJ_ROWS, D_MODEL), lambda i: (i, 0))
    col_spec = pl.BlockSpec((D_MODEL, PROJ_ROWS), lambda i: (0, i))
    args = (h, g_q.reshape(1, D_MODEL), w_q.astype(BF16), g_kv.reshape(1, D_MODEL),
            w_k.T.astype(BF16), w_v.astype(BF16))
    return pl.pallas_call(
        _qkv_kernel,
        name="qkv_proj",
        grid=(seq // PROJ_ROWS,),
        in_specs=[row_spec] + [_const_spec(a.shape) for a in args[1:]],
        out_specs=[row_spec, col_spec, row_spec],
        out_shape=[jax.ShapeDtypeStruct((seq, D_MODEL), BF16),
                   jax.ShapeDtypeStruct((D_MODEL, seq), BF16),
                   jax.ShapeDtypeStruct((seq, D_MODEL), BF16)],
        compiler_params=pltpu.CompilerParams(
            dimension_semantics=("arbitrary",), vmem_limit_bytes=_vmem_limit(40 * 1024 * 1024)),
    )(*args)


def _attn_kernel(q_ref, kt_ref, v_ref, o_ref, qm_ref, acc_ref, lr_ref):
    tile = pl.program_id(1)
    lane = lax.broadcasted_iota(jnp.int32, (ATT_KEYS, V7X_LANES), 1)
    head0 = lane < HEAD_DIM
    srow = lax.broadcasted_iota(jnp.int32, (ATT_PAIR_ROWS, ATT_KEYS), 0)
    skey = lax.broadcasted_iota(jnp.int32, (ATT_PAIR_ROWS, ATT_KEYS), 1)
    causal = jnp.where(srow >= ATT_KEYS, srow - ATT_KEYS, srow) > skey
    trow = lax.broadcasted_iota(jnp.int32, (ATT_KEYS, 2 * ATT_KEYS), 0)
    tcol = lax.broadcasted_iota(jnp.int32, (ATT_KEYS, 2 * ATT_KEYS), 1)
    tri_ones = jnp.where((trow >= tcol) | (tcol >= ATT_KEYS), 1.0, 0.0).astype(BF16)

    for qb in range(ATT_QBLOCKS):
        q = q_ref[qb * ATT_KEYS:(qb + 1) * ATT_KEYS, :]
        zero = jnp.zeros_like(q)
        qm_ref[qb] = jnp.concatenate([jnp.where(head0, q, zero), jnp.where(head0, zero, q)], axis=0)

    half = ATT_KEYS // 2
    early = [slice(hh * ATT_KEYS, hh * ATT_KEYS + half) for hh in range(HEADS_PER_STEP)]
    late = [slice(hh * ATT_KEYS + half, (hh + 1) * ATT_KEYS) for hh in range(HEADS_PER_STEP)]

    def pick(ref, qb, parts):
        return ref[qb] if parts is None else jnp.concatenate([ref[qb, p, :] for p in parts], axis=0)

    def step(kv_blocks, keep, state, parts=None):
        nrows = ATT_PAIR_ROWS if parts is None else ATT_PAIR_ROWS // 2
        starts = [pl.multiple_of(kb * ATT_KEYS, ATT_KEYS) for kb in kv_blocks]
        zs = [_dot(pick(qm_ref, qb, parts), kt_ref[:, pl.ds(starts[qb], ATT_KEYS)]) for qb in range(ATT_QBLOCKS)]
        if keep is not None:
            zs = [jnp.where(keep[qb], zs[qb], ATT_MASKED_LOGIT) for qb in range(ATT_QBLOCKS)]
        sps = [jnp.maximum(z, 0.0) + jnp.log2(1.0 + jnp.exp2(_neg_abs(z))) for z in zs]
        sums = _dot(jnp.concatenate(sps, axis=0).astype(BF16), tri_ones)
        out = []
        for qb in range(ATT_QBLOCKS):
            rows = slice(qb * nrows, (qb + 1) * nrows)
            logw = zs[qb] - sums[rows, :ATT_KEYS]
            total = sums[rows, ATT_KEYS:]
            if state is not None:
                acc, log_rest = state[qb]
                logw = logw + log_rest
            pv = _dot(jnp.exp2(logw).astype(BF16), v_ref[pl.ds(starts[qb], ATT_KEYS), :])
            out.append((pv, -total) if state is None else (acc + pv, log_rest - total))
        return out

    def load_state(parts=None):
        return [(pick(acc_ref, qb, parts), pick(lr_ref, qb, parts)) for qb in range(ATT_QBLOCKS)]

    def store_state(state):
        top = None
        for qb, (acc, log_rest) in enumerate(state):
            acc_ref[qb] = acc
            lr_ref[qb] = log_rest
            top = log_rest if top is None else jnp.maximum(top, log_rest)
        top_early = jnp.max(jnp.maximum(top[early[0]], top[early[1]]))
        top_late = jnp.max(jnp.maximum(top[late[0]], top[late[1]]))
        return top_early, top_late

    def store_early(state):
        top = None
        for qb, (acc, log_rest) in enumerate(state):
            for n, p in enumerate(early):
                acc_ref[qb, p, :] = acc[n * half:(n + 1) * half]
                lr_ref[qb, p, :] = log_rest[n * half:(n + 1) * half]
            top = log_rest if top is None else jnp.maximum(top, log_rest)
        return jnp.max(top)

    first_block = tile * ATT_QBLOCKS
    state = step([first_block + qb for qb in range(ATT_QBLOCKS)], [causal] * ATT_QBLOCKS, None)
    before = [first_block + qb - 1 for qb in range(ATT_QBLOCKS)]
    tops = store_state(step([jnp.maximum(kb, 0) for kb in before], [kb >= 0 for kb in before], state))

    def walk_full(carry):
        s = carry[0]
        return (s + 1,) + store_state(step([first_block + qb - s for qb in range(ATT_QBLOCKS)], None, load_state()))

    def walk_early(carry):
        s, _, top_late = carry
        state = step([first_block + qb - s for qb in range(ATT_QBLOCKS)], None, load_state(early), early)
        return s + 1, store_early(state), top_late

    def walk_ragged(carry):
        s = carry[0]
        blocks = [first_block + qb - s for qb in range(ATT_QBLOCKS)]
        state = step([jnp.maximum(kb, 0) for kb in blocks], [kb >= 0 for kb in blocks], load_state())
        return (s + 1,) + store_state(state)

    carry = (jnp.int32(2),) + tops
    carry = lax.while_loop(lambda c: (c[0] <= first_block) & (c[2] > ATT_LOG2_CUTOFF), walk_full, carry)
    carry = lax.while_loop(lambda c: (c[0] <= first_block) & (c[1] > ATT_LOG2_CUTOFF), walk_early, carry)
    lax.while_loop(lambda c: (c[0] < first_block + ATT_QBLOCKS) & (jnp.maximum(c[1], c[2]) > ATT_LOG2_CUTOFF),
                   walk_ragged, carry)

    for qb in range(ATT_QBLOCKS):
        acc = acc_ref[qb]
        out = jnp.where(head0, acc[:ATT_KEYS], acc[ATT_KEYS:])
        o_ref[qb * ATT_KEYS:(qb + 1) * ATT_KEYS, :] = out.astype(BF16)


def _attention(q, kt, v):
    seq = q.shape[0]
    assert HEADS_PER_STEP == 2
    state = pltpu.VMEM((ATT_QBLOCKS, ATT_PAIR_ROWS, ATT_KEYS), F32)
    stacked_q = pltpu.VMEM((ATT_QBLOCKS, ATT_PAIR_ROWS, V7X_LANES), BF16)
    return pl.pallas_call(
        _attn_kernel,
        name="stickbreaking_attention",
        grid=(N_HEADS // HEADS_PER_STEP, seq // ATT_ROWS),
        in_specs=[pl.BlockSpec((ATT_ROWS, V7X_LANES), lambda p, i: (i, p)),
                  pl.BlockSpec((V7X_LANES, seq), lambda p, i: (p, 0)),
                  pl.BlockSpec((seq, V7X_LANES), lambda p, i: (0, p))],
        out_specs=pl.BlockSpec((ATT_ROWS, V7X_LANES), lambda p, i: (i, p)),
        out_shape=jax.ShapeDtypeStruct((seq, D_MODEL), BF16),
        scratch_shapes=[stacked_q, state, state],
        compiler_params=pltpu.CompilerParams(
            dimension_semantics=("arbitrary", "arbitrary"), vmem_limit_bytes=_vmem_limit(40 * 1024 * 1024)),
    )(q, kt, v)


def kernel(x, norm_mix_pre, norm_mix_post, norm_ffn_pre, norm_ffn_post, w_ffn_in, w_ffn_out, s5_log_dt, s5_a_re, s5_a_im, s5_b_re, s5_b_im, s5_c_re, s5_c_im, s5_d, s5_w_glu, s5_b_glu, kv_norm, w_k, w_v, w_q, w_o):
    bsz, seq, _ = x.shape
    assert bsz == 1 and seq % S5_ROWS == 0 and seq % ATT_ROWS == 0 and seq % FFN_ROWS == 0
    blocks = seq // S5_ROWS
    h = x.reshape(blocks, V7X_SUBLANES, S5_SEG, D_MODEL).transpose(0, 2, 1, 3).reshape(seq, D_MODEL)
    s5_params = jax.vmap(_s5_params)(s5_log_dt, s5_a_re, s5_a_im, s5_b_re, s5_b_im, s5_c_re, s5_c_im)
    for layer in range(N_A_LAYERS):
        params = tuple(p[layer] for p in s5_params)
        h = _s5_layer(h, norm_mix_pre[layer], norm_mix_post[layer], params, s5_d[layer],
                      s5_w_glu[layer], s5_b_glu[layer])
        h = _ffn_layer(h, norm_ffn_pre[layer], norm_ffn_post[layer], w_ffn_in[layer], w_ffn_out[layer])
    h = h.reshape(blocks, S5_SEG, V7X_SUBLANES, D_MODEL).transpose(0, 2, 1, 3).reshape(seq, D_MODEL)

    q, kt, v = _qkv_proj(h, norm_mix_pre[N_A_LAYERS], w_q[0], kv_norm, w_k, w_v)
    for layer in range(N_A_LAYERS, DEPTH):
        b = layer - N_A_LAYERS
        att = _attention(q, kt, v)
        next_q = (norm_mix_pre[layer + 1], w_q[b + 1]) if layer + 1 < DEPTH else None
        out = _ffn_layer(h, norm_ffn_pre[layer], norm_ffn_post[layer], w_ffn_in[layer], w_ffn_out[layer],
                         att=att, w_o=w_o[b], g_mix=norm_mix_post[layer], next_q=next_q)
        h, q = out if next_q is not None else (out, None)
    return h.reshape(bsz, seq, D_MODEL)
```

```python
import functools
import math

import jax
import jax.numpy as jnp
from jax import lax
from jax.experimental import pallas as pl
from jax.experimental.pallas import tpu as pltpu

F32 = jnp.float32
BF16 = jnp.bfloat16

D_MODEL = 1024
DEPTH = 4
N_A_LAYERS = DEPTH // 2
SSM_GROUP = 16
SSM_GROUPS = D_MODEL // SSM_GROUP
SSM_STATE = 64
N_HEADS = 16
HEAD_DIM = D_MODEL // N_HEADS
D_FF = 2816
EPS = 1e-6

V7X_LANES = 128
V7X_SUBLANES = 8
BF16_ROWS = 2 * V7X_SUBLANES
V7X_MXU_DIM = 256
V7X_VMEM_BYTES = 64 * 1024 * 1024

S5_KTILES = D_MODEL // V7X_MXU_DIM
S5_TILE_GROUPS = V7X_MXU_DIM // SSM_GROUP
S5_TILE_STATES = S5_TILE_GROUPS * SSM_STATE
S5_SEG = 64
S5_ROWS = V7X_SUBLANES * S5_SEG
S5_COLS = 512

FFN_ROWS = 512
PROJ_ROWS = 512

ATT_KEYS = 128
ATT_QBLOCKS = 4
ATT_ROWS = ATT_QBLOCKS * ATT_KEYS
HEADS_PER_STEP = V7X_LANES // HEAD_DIM
ATT_PAIR_ROWS = HEADS_PER_STEP * ATT_KEYS
LOG2_E = math.log2(math.e)
ATT_LOG2_CUTOFF = -104.0 * LOG2_E


def _vmem_limit(nbytes):
    return int(min(nbytes, V7X_VMEM_BYTES - 6 * 1024 * 1024))


def _rms(x, g):
    return x * lax.rsqrt(jnp.mean(x * x, axis=-1, keepdims=True) + EPS) * g


def _dot(a, b):
    return jnp.dot(a, b, preferred_element_type=F32)


def _neg_abs(x):
    return -jnp.abs(x)


def _const_spec(shape):
    zeros = (0,) * len(shape)
    return pl.BlockSpec(shape, lambda *_: zeros, pipeline_mode=pl.Buffered(1))


def _s5_kernel(h_ref, gpre_ref, gpost_ref, wb_ref, wc_ref, lre_ref, lim_ref, mre_ref, mim_ref,
               d_ref, wglu_ref, bglu_ref, o_ref,
               xre_ref, xim_ref, xb_ref, cre_ref, cim_ref, y_ref):
    @pl.when(pl.program_id(0) == 0)
    def _():
        cre_ref[...] = jnp.zeros_like(cre_ref)
        cim_ref[...] = jnp.zeros_like(cim_ref)

    h = h_ref[...]
    u = _rms(h, gpre_ref[...])
    u_bf = u.astype(BF16)
    zeros = jnp.zeros((V7X_SUBLANES, S5_COLS), F32)
    row = lax.broadcasted_iota(jnp.int32, (V7X_SUBLANES, S5_COLS), 0)

    for kt in range(S5_KTILES):
        slot = kt % 2
        bu = _dot(u_bf[:, kt * V7X_MXU_DIM:(kt + 1) * V7X_MXU_DIM], wb_ref[kt])
        xre_ref[slot] = bu[:, :S5_TILE_STATES]
        xim_ref[slot] = bu[:, S5_TILE_STATES:]

        for cg in range(S5_TILE_STATES // S5_COLS):
            cs = slice(cg * S5_COLS, (cg + 1) * S5_COLS)
            lre = jnp.broadcast_to(lre_ref[kt, :, cs], (V7X_SUBLANES, S5_COLS))
            lim = jnp.broadcast_to(lim_ref[kt, :, cs], (V7X_SUBLANES, S5_COLS))

            er, ei = zeros, zeros
            for i in range(S5_SEG):
                rows = pl.ds(i * V7X_SUBLANES, V7X_SUBLANES)
                er, ei = (lre * er - lim * ei + xre_ref[slot, rows, cs],
                          lre * ei + lim * er + xim_ref[slot, rows, cs])
                xre_ref[slot, rows, cs] = er
                xim_ref[slot, rows, cs] = ei

            mre = mre_ref[kt, :, cs]
            mim = mim_ref[kt, :, cs]
            cur_r = cre_ref[kt, :, cs]
            cur_i = cim_ref[kt, :, cs]
            xin_r = zeros
            xin_i = zeros
            for r in range(V7X_SUBLANES):
                xin_r = jnp.where(row == r, cur_r, xin_r)
                xin_i = jnp.where(row == r, cur_i, xin_i)
                nxt_r = mre * cur_r - mim * cur_i + er[r:r + 1, :]
                nxt_i = mre * cur_i + mim * cur_r + ei[r:r + 1, :]
                cur_r, cur_i = nxt_r, nxt_i
            cre_ref[kt, :, cs] = cur_r
            cim_ref[kt, :, cs] = cur_i

            cs_im = slice(S5_TILE_STATES + cg * S5_COLS, S5_TILE_STATES + (cg + 1) * S5_COLS)

            cr, ci = xin_r, xin_i
            for j in range(S5_ROWS // BF16_ROWS):
                full_r, full_i = [], []
                for half in range(BF16_ROWS // V7X_SUBLANES):
                    cr, ci = lre * cr - lim * ci, lre * ci + lim * cr
                    rows = pl.ds(j * BF16_ROWS + half * V7X_SUBLANES, V7X_SUBLANES)
                    full_r.append(xre_ref[slot, rows, cs] + cr)
                    full_i.append(xim_ref[slot, rows, cs] + ci)
                packed = pl.ds(j * BF16_ROWS, BF16_ROWS)
                xb_ref[slot, packed, cs] = jnp.concatenate(full_r, axis=0).astype(BF16)
                xb_ref[slot, packed, cs_im] = jnp.concatenate(full_i, axis=0).astype(BF16)

        y_ref[:, kt * V7X_MXU_DIM:(kt + 1) * V7X_MXU_DIM] = _dot(xb_ref[slot], wc_ref[kt])

    y = y_ref[...] + d_ref[...] * u
    a = jax.nn.gelu(y).astype(BF16)
    z = _dot(a, wglu_ref[...]) + bglu_ref[...]
    mix = z[:, :D_MODEL] * jax.nn.sigmoid(z[:, D_MODEL:])
    o_ref[...] = h + _rms(mix, gpost_ref[...])


def _s5_params(log_dt, a_re, a_im, b_re, b_im, c_re, c_im):
    dt = jnp.exp(log_dt.astype(F32))[:, None]
    a_re = a_re.astype(F32)
    a_im = a_im.astype(F32)
    mag = jnp.exp(a_re * dt)
    lam_re = mag * jnp.cos(a_im * dt)
    lam_im = mag * jnp.sin(a_im * dt)
    den = a_re * a_re + a_im * a_im
    nr = lam_re - 1.0
    coef_re = (nr * a_re + lam_im * a_im) / den
    coef_im = (lam_im * a_re - nr * a_im) / den
    bb_re = coef_re[..., None] * b_re - coef_im[..., None] * b_im
    bb_im = coef_re[..., None] * b_im + coef_im[..., None] * b_re

    eye = jnp.eye(S5_TILE_GROUPS, dtype=F32)

    def in_proj(b):
        b = b.reshape(S5_KTILES, S5_TILE_GROUPS, SSM_STATE, SSM_GROUP)
        w = jnp.einsum('kgpc,gh->kgchp', b, eye)
        return w.reshape(S5_KTILES, V7X_MXU_DIM, S5_TILE_STATES)

    def out_proj(c):
        c = c.astype(F32).reshape(S5_KTILES, S5_TILE_GROUPS, SSM_GROUP, SSM_STATE)
        w = jnp.einsum('kgcp,gh->kgphc', c, eye)
        return w.reshape(S5_KTILES, S5_TILE_STATES, V7X_MXU_DIM)

    wb = jnp.concatenate([in_proj(bb_re), in_proj(bb_im)], axis=2).astype(BF16)
    wc = jnp.concatenate([out_proj(c_re), -out_proj(c_im)], axis=1).astype(BF16)

    def tile(v):
        return v.reshape(S5_KTILES, 1, S5_TILE_STATES)

    seg_mag = jnp.exp(float(S5_SEG) * a_re * dt)
    m_re = seg_mag * jnp.cos(float(S5_SEG) * a_im * dt)
    m_im = seg_mag * jnp.sin(float(S5_SEG) * a_im * dt)
    return wb, wc, tile(lam_re), tile(lam_im), tile(m_re), tile(m_im)


def _s5_layer(h, g_pre, g_post, params, d_skip, w_glu, b_glu):
    seq = h.shape[0]
    row_spec = pl.BlockSpec((S5_ROWS, D_MODEL), lambda i: (i, 0))
    args = (h, g_pre.reshape(1, D_MODEL), g_post.reshape(1, D_MODEL),
            *params, d_skip.reshape(1, D_MODEL).astype(F32), w_glu.astype(BF16),
            b_glu.reshape(1, 2 * D_MODEL).astype(F32))
    in_specs = [row_spec] + [_const_spec(a.shape) for a in args[1:]]
    return pl.pallas_call(
        _s5_kernel,
        name="s5_mixer",
        grid=(seq // S5_ROWS,),
        in_specs=in_specs,
        out_specs=row_spec,
        out_shape=jax.ShapeDtypeStruct((seq, D_MODEL), F32),
        scratch_shapes=[
            pltpu.VMEM((2, S5_ROWS, S5_TILE_STATES), F32),
            pltpu.VMEM((2, S5_ROWS, S5_TILE_STATES), F32),
            pltpu.VMEM((2, S5_ROWS, 2 * S5_TILE_STATES), BF16),
            pltpu.VMEM((S5_KTILES, 1, S5_TILE_STATES), F32),
            pltpu.VMEM((S5_KTILES, 1, S5_TILE_STATES), F32),
            pltpu.VMEM((S5_ROWS, D_MODEL), F32),
        ],
        compiler_params=pltpu.CompilerParams(
            dimension_semantics=("arbitrary",), vmem_limit_bytes=_vmem_limit(56 * 1024 * 1024)),
    )(*args)


def _ffn_kernel(*refs, has_att, has_next_q):
    refs = list(refs)
    h = refs.pop(0)[...]
    if has_att:
        att_ref, wo_ref, gmix_ref = refs[:3]
        del refs[:3]
        h = h + _rms(_dot(att_ref[...], wo_ref[...]), gmix_ref[...])
    gpre_ref, gpost_ref, win_ref, wout_ref = refs[:4]
    del refs[:4]
    if has_next_q:
        gq_ref, wq_ref = refs[:2]
        del refs[:2]
    o_ref = refs.pop(0)
    xn = _rms(h, gpre_ref[...]).astype(BF16)
    hd = _dot(xn, win_ref[...])
    a = (jax.nn.silu(hd[:, :D_FF]) * hd[:, D_FF:]).astype(BF16)
    out = h + _rms(_dot(a, wout_ref[...]), gpost_ref[...])
    o_ref[...] = out
    if has_next_q:
        q_ref = refs.pop(0)
        q_ref[...] = _q_of(out, gq_ref, wq_ref)


def _ffn_layer(h, g_pre, g_post, w_in, w_out, att=None, w_o=None, g_mix=None, next_q=None):
    seq = h.shape[0]
    row_spec = pl.BlockSpec((FFN_ROWS, D_MODEL), lambda i: (i, 0))
    args, specs = [h], [row_spec]
    if att is not None:
        args += [att, w_o.astype(BF16), g_mix.reshape(1, D_MODEL)]
        specs += [row_spec, _const_spec((D_MODEL, D_MODEL)), _const_spec((1, D_MODEL))]
    consts = [g_pre.reshape(1, D_MODEL), g_post.reshape(1, D_MODEL), w_in.astype(BF16), w_out.astype(BF16)]
    if next_q is not None:
        consts += [next_q[0].reshape(1, D_MODEL), next_q[1].astype(BF16)]
    args += consts
    specs += [_const_spec(a.shape) for a in consts]
    out_specs, out_shape = [row_spec], [jax.ShapeDtypeStruct((seq, D_MODEL), F32)]
    if next_q is not None:
        out_specs.append(row_spec)
        out_shape.append(jax.ShapeDtypeStruct((seq, D_MODEL), BF16))
    outs = pl.pallas_call(
        functools.partial(_ffn_kernel, has_att=att is not None, has_next_q=next_q is not None),
        name="ffn",
        grid=(seq // FFN_ROWS,),
        in_specs=specs,
        out_specs=out_specs,
        out_shape=out_shape,
        compiler_params=pltpu.CompilerParams(
            dimension_semantics=("arbitrary",), vmem_limit_bytes=_vmem_limit(56 * 1024 * 1024)),
    )(*args)
    return outs[0] if next_q is None else tuple(outs)


def _q_of(h, gq_ref, wq_ref):
    u = _rms(h, gq_ref[...]).astype(BF16)
    return (_dot(u, wq_ref[...]) * (HEAD_DIM ** -0.5 * LOG2_E)).astype(BF16)


def _qkv_kernel(h_ref, gq_ref, wq_ref, gkv_ref, wkt_ref, wv_ref, q_ref, kt_ref, v_ref):
    h = h_ref[...]
    q_ref[...] = _q_of(h, gq_ref, wq_ref)
    hk = _rms(h, gkv_ref[...]).astype(BF16)
    kt_ref[...] = lax.dot_general(wkt_ref[...], hk, (((1,), (1,)), ((), ())),
                                  preferred_element_type=F32).astype(BF16)
    v_ref[...] = _dot(hk, wv_ref[...]).astype(BF16)


def _qkv_proj(h, g_q, w_q, g_kv, w_k, w_v):
    seq = h.shape[0]
    row_spec = pl.BlockSpec((PROJ_ROWS, D_MODEL), lambda i: (i, 0))
    col_spec = pl.BlockSpec((D_MODEL, PROJ_ROWS), lambda i: (0, i))
    args = (h, g_q.reshape(1, D_MODEL), w_q.astype(BF16), g_kv.reshape(1, D_MODEL),
            w_k.T.astype(BF16), w_v.astype(BF16))
    return pl.pallas_call(
        _qkv_kernel,
        name="qkv_proj",
        grid=(seq // PROJ_ROWS,),
        in_specs=[row_spec] + [_const_spec(a.shape) for a in args[1:]],
        out_specs=[row_spec, col_spec, row_spec],
        out_shape=[jax.ShapeDtypeStruct((seq, D_MODEL), BF16),
                   jax.ShapeDtypeStruct((D_MODEL, seq), BF16),
                   jax.ShapeDtypeStruct((seq, D_MODEL), BF16)],
        compiler_params=pltpu.CompilerParams(
            dimension_semantics=("arbitrary",), vmem_limit_bytes=_vmem_limit(40 * 1024 * 1024)),
    )(*args)


def _attn_kernel(q_ref, kt_ref, v_ref, o_ref, qm_ref, acc_ref, lr_ref):
    tile = pl.program_id(1)
    lane = lax.broadcasted_iota(jnp.int32, (ATT_KEYS, V7X_LANES), 1)
    head0 = lane < HEAD_DIM
    srow = lax.broadcasted_iota(jnp.int32, (ATT_PAIR_ROWS, ATT_KEYS), 0)
    skey = lax.broadcasted_iota(jnp.int32, (ATT_PAIR_ROWS, ATT_KEYS), 1)
    causal = jnp.where(srow >= ATT_KEYS, srow - ATT_KEYS, srow) > skey
    trow = lax.broadcasted_iota(jnp.int32, (ATT_KEYS, 2 * ATT_KEYS), 0)
    tcol = lax.broadcasted_iota(jnp.int32, (ATT_KEYS, 2 * ATT_KEYS), 1)
    tri_ones = jnp.where((trow >= tcol) | (tcol >= ATT_KEYS), 1.0, 0.0).astype(BF16)

    for qb in range(ATT_QBLOCKS):
        q = q_ref[qb * ATT_KEYS:(qb + 1) * ATT_KEYS, :]
        zero = jnp.zeros_like(q)
        qm_ref[qb] = jnp.concatenate([jnp.where(head0, q, zero), jnp.where(head0, zero, q)], axis=0)

    half = ATT_KEYS // 2
    early = [slice(hh * ATT_KEYS, hh * ATT_KEYS + half) for hh in range(HEADS_PER_STEP)]
    late = [slice(hh * ATT_KEYS + half, (hh + 1) * ATT_KEYS) for hh in range(HEADS_PER_STEP)]

    def pick(ref, qb, parts):
        return ref[qb] if parts is None else jnp.concatenate([ref[qb, p, :] for p in parts], axis=0)

    def step(kv_blocks, keep, state, parts=None):
        nrows = ATT_PAIR_ROWS if parts is None else ATT_PAIR_ROWS // 2
        starts = [pl.multiple_of(kb * ATT_KEYS, ATT_KEYS) for kb in kv_blocks]
        zs = [_dot(pick(qm_ref, qb, parts), kt_ref[:, pl.ds(starts[qb], ATT_KEYS)]) for qb in range(ATT_QBLOCKS)]
        sps = []
        for qb in range(ATT_QBLOCKS):
            sp = jnp.maximum(zs[qb], 0.0) + jnp.log2(1.0 + jnp.exp2(_neg_abs(zs[qb])))
            sps.append(sp if keep is None else jnp.where(keep[qb], sp, 0.0))
        sums = _dot(jnp.concatenate(sps, axis=0).astype(BF16), tri_ones)
        out = []
        for qb in range(ATT_QBLOCKS):
            rows = slice(qb * nrows, (qb + 1) * nrows)
            logw = zs[qb] - sums[rows, :ATT_KEYS]
            total = sums[rows, ATT_KEYS:]
            if state is not None:
                acc, log_rest = state[qb]
                logw = logw + log_rest
            w = jnp.exp2(logw)
            if keep is not None:
                w = jnp.where(keep[qb], w, 0.0)
            pv = _dot(w.astype(BF16), v_ref[pl.ds(starts[qb], ATT_KEYS), :])
            out.append((pv, -total) if state is None else (acc + pv, log_rest - total))
        return out

    def load_state(parts=None):
        return [(pick(acc_ref, qb, parts), pick(lr_ref, qb, parts)) for qb in range(ATT_QBLOCKS)]

    def store_state(state):
        top = None
        for qb, (acc, log_rest) in enumerate(state):
            acc_ref[qb] = acc
            lr_ref[qb] = log_rest
            top = log_rest if top is None else jnp.maximum(top, log_rest)
        top_early = jnp.max(jnp.maximum(top[early[0]], top[early[1]]))
        top_late = jnp.max(jnp.maximum(top[late[0]], top[late[1]]))
        return top_early, top_late

    def store_early(state):
        top = None
        for qb, (acc, log_rest) in enumerate(state):
            for n, p in enumerate(early):
                acc_ref[qb, p, :] = acc[n * half:(n + 1) * half]
                lr_ref[qb, p, :] = log_rest[n * half:(n + 1) * half]
            top = log_rest if top is None else jnp.maximum(top, log_rest)
        return jnp.max(top)

    first_block = tile * ATT_QBLOCKS
    state = step([first_block + qb for qb in range(ATT_QBLOCKS)], [causal] * ATT_QBLOCKS, None)
    before = [first_block + qb - 1 for qb in range(ATT_QBLOCKS)]
    tops = store_state(step([jnp.maximum(kb, 0) for kb in before], [kb >= 0 for kb in before], state))

    def walk_full(carry):
        s = carry[0]
        return (s + 1,) + store_state(step([first_block + qb - s for qb in range(ATT_QBLOCKS)], None, load_state()))

    def walk_early(carry):
        s, _, top_late = carry
        state = step([first_block + qb - s for qb in range(ATT_QBLOCKS)], None, load_state(early), early)
        return s + 1, store_early(state), top_late

    def walk_ragged(carry):
        s = carry[0]
        blocks = [first_block + qb - s for qb in range(ATT_QBLOCKS)]
        state = step([jnp.maximum(kb, 0) for kb in blocks], [kb >= 0 for kb in blocks], load_state())
        return (s + 1,) + store_state(state)

    carry = (jnp.int32(2),) + tops
    carry = lax.while_loop(lambda c: (c[0] <= first_block) & (c[2] > ATT_LOG2_CUTOFF), walk_full, carry)
    carry = lax.while_loop(lambda c: (c[0] <= first_block) & (c[1] > ATT_LOG2_CUTOFF), walk_early, carry)
    lax.while_loop(lambda c: (c[0] < first_block + ATT_QBLOCKS) & (jnp.maximum(c[1], c[2]) > ATT_LOG2_CUTOFF),
                   walk_ragged, carry)

    for qb in range(ATT_QBLOCKS):
        acc = acc_ref[qb]
        out = jnp.where(head0, acc[:ATT_KEYS], acc[ATT_KEYS:])
        o_ref[qb * ATT_KEYS:(qb + 1) * ATT_KEYS, :] = out.astype(BF16)


def _attention(q, kt, v):
    seq = q.shape[0]
    assert HEADS_PER_STEP == 2
    state = pltpu.VMEM((ATT_QBLOCKS, ATT_PAIR_ROWS, ATT_KEYS), F32)
    stacked_q = pltpu.VMEM((ATT_QBLOCKS, ATT_PAIR_ROWS, V7X_LANES), BF16)
    return pl.pallas_call(
        _attn_kernel,
        name="stickbreaking_attention",
        grid=(N_HEADS // HEADS_PER_STEP, seq // ATT_ROWS),
        in_specs=[pl.BlockSpec((ATT_ROWS, V7X_LANES), lambda p, i: (i, p)),
                  pl.BlockSpec((V7X_LANES, seq), lambda p, i: (p, 0)),
                  pl.BlockSpec((seq, V7X_LANES), lambda p, i: (0, p))],
        out_specs=pl.BlockSpec((ATT_ROWS, V7X_LANES), lambda p, i: (i, p)),
        out_shape=jax.ShapeDtypeStruct((seq, D_MODEL), BF16),
        scratch_shapes=[stacked_q, state, state],
        compiler_params=pltpu.CompilerParams(
            dimension_semantics=("arbitrary", "arbitrary"), vmem_limit_bytes=_vmem_limit(40 * 1024 * 1024)),
    )(q, kt, v)


def kernel(x, norm_mix_pre, norm_mix_post, norm_ffn_pre, norm_ffn_post, w_ffn_in, w_ffn_out, s5_log_dt, s5_a_re, s5_a_im, s5_b_re, s5_b_im, s5_c_re, s5_c_im, s5_d, s5_w_glu, s5_b_glu, kv_norm, w_k, w_v, w_q, w_o):
    bsz, seq, _ = x.shape
    assert bsz == 1 and seq % S5_ROWS == 0 and seq % ATT_ROWS == 0 and seq % FFN_ROWS == 0
    blocks = seq // S5_ROWS
    h = x.reshape(blocks, V7X_SUBLANES, S5_SEG, D_MODEL).transpose(0, 2, 1, 3).reshape(seq, D_MODEL)
    s5_params = jax.vmap(_s5_params)(s5_log_dt, s5_a_re, s5_a_im, s5_b_re, s5_b_im, s5_c_re, s5_c_im)
    for layer in range(N_A_LAYERS):
        params = tuple(p[layer] for p in s5_params)
        h = _s5_layer(h, norm_mix_pre[layer], norm_mix_post[layer], params, s5_d[layer],
                      s5_w_glu[layer], s5_b_glu[layer])
        h = _ffn_layer(h, norm_ffn_pre[layer], norm_ffn_post[layer], w_ffn_in[layer], w_ffn_out[layer])
    h = h.reshape(blocks, S5_SEG, V7X_SUBLANES, D_MODEL).transpose(0, 2, 1, 3).reshape(seq, D_MODEL)

    q, kt, v = _qkv_proj(h, norm_mix_pre[N_A_LAYERS], w_q[0], kv_norm, w_k, w_v)
    for layer in range(N_A_LAYERS, DEPTH):
        b = layer - N_A_LAYERS
        att = _attention(q, kt, v)
        next_q = (norm_mix_pre[layer + 1], w_q[b + 1]) if layer + 1 < DEPTH else None
        out = _ffn_layer(h, norm_ffn_pre[layer], norm_ffn_post[layer], w_ffn_in[layer], w_ffn_out[layer],
                         att=att, w_o=w_o[b], g_mix=norm_mix_post[layer], next_q=next_q)
        h, q = out if next_q is not None else (out, None)
    return h.reshape(bsz, seq, D_MODEL)
```

```python
import functools
import math

import jax
import jax.numpy as jnp
from jax import lax
from jax.experimental import pallas as pl
from jax.experimental.pallas import tpu as pltpu

F32 = jnp.float32
BF16 = jnp.bfloat16

D_MODEL = 1024
DEPTH = 4
N_A_LAYERS = DEPTH // 2
SSM_GROUP = 16
SSM_GROUPS = D_MODEL // SSM_GROUP
SSM_STATE = 64
N_HEADS = 16
HEAD_DIM = D_MODEL // N_HEADS
D_FF = 2816
EPS = 1e-6

V7X_LANES = 128
V7X_SUBLANES = 8
BF16_ROWS = 2 * V7X_SUBLANES
V7X_MXU_DIM = 256
V7X_VMEM_BYTES = 64 * 1024 * 1024

S5_KTILES = D_MODEL // V7X_MXU_DIM
S5_TILE_GROUPS = V7X_MXU_DIM // SSM_GROUP
S5_TILE_STATES = S5_TILE_GROUPS * SSM_STATE
S5_SEG = 64
S5_ROWS = V7X_SUBLANES * S5_SEG
S5_COLS = 1024

FFN_ROWS = 512
PROJ_ROWS = 512

ATT_KEYS = 128
ATT_QBLOCKS = 8
ATT_ROWS = ATT_QBLOCKS * ATT_KEYS
HEADS_PER_STEP = V7X_LANES // HEAD_DIM
ATT_PAIR_ROWS = HEADS_PER_STEP * ATT_KEYS
LOG2_E = math.log2(math.e)
ATT_LOG2_CUTOFF = -104.0 * LOG2_E


def _vmem_limit(nbytes):
    return int(min(nbytes, V7X_VMEM_BYTES - 6 * 1024 * 1024))


def _rms(x, g):
    return x * lax.rsqrt(jnp.mean(x * x, axis=-1, keepdims=True) + EPS) * g


def _dot(a, b):
    return jnp.dot(a, b, preferred_element_type=F32)


def _neg_abs(x):
    return -jnp.abs(x)


def _const_spec(shape):
    zeros = (0,) * len(shape)
    return pl.BlockSpec(shape, lambda *_: zeros, pipeline_mode=pl.Buffered(1))


def _s5_kernel(h_ref, gpre_ref, gpost_ref, wb_ref, wc_ref, lre_ref, lim_ref, mre_ref, mim_ref,
               d_ref, wglu_ref, bglu_ref, o_ref,
               xre_ref, xim_ref, xb_ref, cre_ref, cim_ref, y_ref):
    @pl.when(pl.program_id(0) == 0)
    def _():
        cre_ref[...] = jnp.zeros_like(cre_ref)
        cim_ref[...] = jnp.zeros_like(cim_ref)

    h = h_ref[...]
    u = _rms(h, gpre_ref[...])
    u_bf = u.astype(BF16)
    zeros = jnp.zeros((V7X_SUBLANES, S5_COLS), F32)
    row = lax.broadcasted_iota(jnp.int32, (V7X_SUBLANES, S5_COLS), 0)

    for kt in range(S5_KTILES):
        slot = kt % 2
        bu = _dot(u_bf[:, kt * V7X_MXU_DIM:(kt + 1) * V7X_MXU_DIM], wb_ref[kt])
        xre_ref[slot] = bu[:, :S5_TILE_STATES]
        xim_ref[slot] = bu[:, S5_TILE_STATES:]

        for cg in range(S5_TILE_STATES // S5_COLS):
            cs = slice(cg * S5_COLS, (cg + 1) * S5_COLS)
            lre = jnp.broadcast_to(lre_ref[kt, :, cs], (V7X_SUBLANES, S5_COLS))
            lim = jnp.broadcast_to(lim_ref[kt, :, cs], (V7X_SUBLANES, S5_COLS))

            er, ei = zeros, zeros
            for i in range(S5_SEG):
                rows = pl.ds(i * V7X_SUBLANES, V7X_SUBLANES)
                er, ei = (lre * er - lim * ei + xre_ref[slot, rows, cs],
                          lre * ei + lim * er + xim_ref[slot, rows, cs])
                xre_ref[slot, rows, cs] = er
                xim_ref[slot, rows, cs] = ei

            mre = mre_ref[kt, :, cs]
            mim = mim_ref[kt, :, cs]
            cur_r = cre_ref[kt, :, cs]
            cur_i = cim_ref[kt, :, cs]
            xin_r = zeros
            xin_i = zeros
            for r in range(V7X_SUBLANES):
                xin_r = jnp.where(row == r, cur_r, xin_r)
                xin_i = jnp.where(row == r, cur_i, xin_i)
                nxt_r = mre * cur_r - mim * cur_i + er[r:r + 1, :]
                nxt_i = mre * cur_i + mim * cur_r + ei[r:r + 1, :]
                cur_r, cur_i = nxt_r, nxt_i
            cre_ref[kt, :, cs] = cur_r
            cim_ref[kt, :, cs] = cur_i

            cs_im = slice(S5_TILE_STATES + cg * S5_COLS, S5_TILE_STATES + (cg + 1) * S5_COLS)

            cr, ci = xin_r, xin_i
            for j in range(S5_ROWS // BF16_ROWS):
                full_r, full_i = [], []
                for half in range(BF16_ROWS // V7X_SUBLANES):
                    cr, ci = lre * cr - lim * ci, lre * ci + lim * cr
                    rows = pl.ds(j * BF16_ROWS + half * V7X_SUBLANES, V7X_SUBLANES)
                    full_r.append(xre_ref[slot, rows, cs] + cr)
                    full_i.append(xim_ref[slot, rows, cs] + ci)
                packed = pl.ds(j * BF16_ROWS, BF16_ROWS)
                xb_ref[slot, packed, cs] = jnp.concatenate(full_r, axis=0).astype(BF16)
                xb_ref[slot, packed, cs_im] = jnp.concatenate(full_i, axis=0).astype(BF16)

        y_ref[:, kt * V7X_MXU_DIM:(kt + 1) * V7X_MXU_DIM] = _dot(xb_ref[slot], wc_ref[kt])

    y = y_ref[...] + d_ref[...] * u
    a = jax.nn.gelu(y).astype(BF16)
    z = _dot(a, wglu_ref[...]) + bglu_ref[...]
    mix = z[:, :D_MODEL] * jax.nn.sigmoid(z[:, D_MODEL:])
    o_ref[...] = h + _rms(mix, gpost_ref[...])


def _s5_params(log_dt, a_re, a_im, b_re, b_im, c_re, c_im):
    dt = jnp.exp(log_dt.astype(F32))[:, None]
    a_re = a_re.astype(F32)
    a_im = a_im.astype(F32)
    mag = jnp.exp(a_re * dt)
    lam_re = mag * jnp.cos(a_im * dt)
    lam_im = mag * jnp.sin(a_im * dt)
    den = a_re * a_re + a_im * a_im
    nr = lam_re - 1.0
    coef_re = (nr * a_re + lam_im * a_im) / den
    coef_im = (lam_im * a_re - nr * a_im) / den
    bb_re = coef_re[..., None] * b_re - coef_im[..., None] * b_im
    bb_im = coef_re[..., None] * b_im + coef_im[..., None] * b_re

    eye = jnp.eye(S5_TILE_GROUPS, dtype=F32)

    def in_proj(b):
        b = b.reshape(S5_KTILES, S5_TILE_GROUPS, SSM_STATE, SSM_GROUP)
        w = jnp.einsum('kgpc,gh->kgchp', b, eye)
        return w.reshape(S5_KTILES, V7X_MXU_DIM, S5_TILE_STATES)

    def out_proj(c):
        c = c.astype(F32).reshape(S5_KTILES, S5_TILE_GROUPS, SSM_GROUP, SSM_STATE)
        w = jnp.einsum('kgcp,gh->kgphc', c, eye)
        return w.reshape(S5_KTILES, S5_TILE_STATES, V7X_MXU_DIM)

    wb = jnp.concatenate([in_proj(bb_re), in_proj(bb_im)], axis=2).astype(BF16)
    wc = jnp.concatenate([out_proj(c_re), -out_proj(c_im)], axis=1).astype(BF16)

    def tile(v):
        return v.reshape(S5_KTILES, 1, S5_TILE_STATES)

    seg_mag = jnp.exp(float(S5_SEG) * a_re * dt)
    m_re = seg_mag * jnp.cos(float(S5_SEG) * a_im * dt)
    m_im = seg_mag * jnp.sin(float(S5_SEG) * a_im * dt)
    return wb, wc, tile(lam_re), tile(lam_im), tile(m_re), tile(m_im)


def _s5_layer(h, g_pre, g_post, params, d_skip, w_glu, b_glu):
    seq = h.shape[0]
    row_spec = pl.BlockSpec((S5_ROWS, D_MODEL), lambda i: (i, 0))
    args = (h, g_pre.reshape(1, D_MODEL), g_post.reshape(1, D_MODEL),
            *params, d_skip.reshape(1, D_MODEL).astype(F32), w_glu.astype(BF16),
            b_glu.reshape(1, 2 * D_MODEL).astype(F32))
    in_specs = [row_spec] + [_const_spec(a.shape) for a in args[1:]]
    return pl.pallas_call(
        _s5_kernel,
        name="s5_mixer",
        grid=(seq // S5_ROWS,),
        in_specs=in_specs,
        out_specs=row_spec,
        out_shape=jax.ShapeDtypeStruct((seq, D_MODEL), F32),
        scratch_shapes=[
            pltpu.VMEM((2, S5_ROWS, S5_TILE_STATES), F32),
            pltpu.VMEM((2, S5_ROWS, S5_TILE_STATES), F32),
            pltpu.VMEM((2, S5_ROWS, 2 * S5_TILE_STATES), BF16),
            pltpu.VMEM((S5_KTILES, 1, S5_TILE_STATES), F32),
            pltpu.VMEM((S5_KTILES, 1, S5_TILE_STATES), F32),
            pltpu.VMEM((S5_ROWS, D_MODEL), F32),
        ],
        compiler_params=pltpu.CompilerParams(
            dimension_semantics=("arbitrary",), vmem_limit_bytes=_vmem_limit(56 * 1024 * 1024)),
    )(*args)


def _ffn_kernel(*refs, has_att, has_next_q):
    refs = list(refs)
    h = refs.pop(0)[...]
    if has_att:
        att_ref, wo_ref, gmix_ref = refs[:3]
        del refs[:3]
        h = h + _rms(_dot(att_ref[...], wo_ref[...]), gmix_ref[...])
    gpre_ref, gpost_ref, win_ref, wout_ref = refs[:4]
    del refs[:4]
    if has_next_q:
        gq_ref, wq_ref = refs[:2]
        del refs[:2]
    o_ref = refs.pop(0)
    xn = _rms(h, gpre_ref[...]).astype(BF16)
    hd = _dot(xn, win_ref[...])
    a = (jax.nn.silu(hd[:, :D_FF]) * hd[:, D_FF:]).astype(BF16)
    out = h + _rms(_dot(a, wout_ref[...]), gpost_ref[...])
    o_ref[...] = out
    if has_next_q:
        q_ref = refs.pop(0)
        q_ref[...] = _q_of(out, gq_ref, wq_ref)


def _ffn_layer(h, g_pre, g_post, w_in, w_out, att=None, w_o=None, g_mix=None, next_q=None):
    seq = h.shape[0]
    row_spec = pl.BlockSpec((FFN_ROWS, D_MODEL), lambda i: (i, 0))
    args, specs = [h], [row_spec]
    if att is not None:
        args += [att, w_o.astype(BF16), g_mix.reshape(1, D_MODEL)]
        specs += [row_spec, _const_spec((D_MODEL, D_MODEL)), _const_spec((1, D_MODEL))]
    consts = [g_pre.reshape(1, D_MODEL), g_post.reshape(1, D_MODEL), w_in.astype(BF16), w_out.astype(BF16)]
    if next_q is not None:
        consts += [next_q[0].reshape(1, D_MODEL), next_q[1].astype(BF16)]
    args += consts
    specs += [_const_spec(a.shape) for a in consts]
    out_specs, out_shape = [row_spec], [jax.ShapeDtypeStruct((seq, D_MODEL), F32)]
    if next_q is not None:
        out_specs.append(row_spec)
        out_shape.append(jax.ShapeDtypeStruct((seq, D_MODEL), BF16))
    outs = pl.pallas_call(
        functools.partial(_ffn_kernel, has_att=att is not None, has_next_q=next_q is not None),
        name="ffn",
        grid=(seq // FFN_ROWS,),
        in_specs=specs,
        out_specs=out_specs,
        out_shape=out_shape,
        compiler_params=pltpu.CompilerParams(
            dimension_semantics=("arbitrary",), vmem_limit_bytes=_vmem_limit(56 * 1024 * 1024)),
    )(*args)
    return outs[0] if next_q is None else tuple(outs)


def _q_of(h, gq_ref, wq_ref):
    u = _rms(h, gq_ref[...]).astype(BF16)
    return (_dot(u, wq_ref[...]) * (HEAD_DIM ** -0.5 * LOG2_E)).astype(BF16)


def _qkv_kernel(h_ref, gq_ref, wq_ref, gkv_ref, wkt_ref, wv_ref, q_ref, kt_ref, v_ref):
    h = h_ref[...]
    q_ref[...] = _q_of(h, gq_ref, wq_ref)
    hk = _rms(h, gkv_ref[...]).astype(BF16)
    kt_ref[...] = lax.dot_general(wkt_ref[...], hk, (((1,), (1,)), ((), ())),
                                  preferred_element_type=F32).astype(BF16)
    v_ref[...] = _dot(hk, wv_ref[...]).astype(BF16)


def _qkv_proj(h, g_q, w_q, g_kv, w_k, w_v):
    seq = h.shape[0]
    row_spec = pl.BlockSpec((PROJ_ROWS, D_MODEL), lambda i: (i, 0))
    col_spec = pl.BlockSpec((D_MODEL, PROJ_ROWS), lambda i: (0, i))
    args = (h, g_q.reshape(1, D_MODEL), w_q.astype(BF16), g_kv.reshape(1, D_MODEL),
            w_k.T.astype(BF16), w_v.astype(BF16))
    return pl.pallas_call(
        _qkv_kernel,
        name="qkv_proj",
        grid=(seq // PROJ_ROWS,),
        in_specs=[row_spec] + [_const_spec(a.shape) for a in args[1:]],
        out_specs=[row_spec, col_spec, row_spec],
        out_shape=[jax.ShapeDtypeStruct((seq, D_MODEL), BF16),
                   jax.ShapeDtypeStruct((D_MODEL, seq), BF16),
                   jax.ShapeDtypeStruct((seq, D_MODEL), BF16)],
        compiler_params=pltpu.CompilerParams(
            dimension_semantics=("arbitrary",), vmem_limit_bytes=_vmem_limit(40 * 1024 * 1024)),
    )(*args)


def _attn_kernel(q_ref, kt_ref, v_ref, o_ref, qm_ref, acc_ref, lr_ref):
    tile = pl.program_id(1)
    lane = lax.broadcasted_iota(jnp.int32, (ATT_KEYS, V7X_LANES), 1)
    head0 = lane < HEAD_DIM
    srow = lax.broadcasted_iota(jnp.int32, (ATT_PAIR_ROWS, ATT_KEYS), 0)
    skey = lax.broadcasted_iota(jnp.int32, (ATT_PAIR_ROWS, ATT_KEYS), 1)
    causal = jnp.where(srow >= ATT_KEYS, srow - ATT_KEYS, srow) > skey
    trow = lax.broadcasted_iota(jnp.int32, (ATT_KEYS, 2 * ATT_KEYS), 0)
    tcol = lax.broadcasted_iota(jnp.int32, (ATT_KEYS, 2 * ATT_KEYS), 1)
    tri_ones = jnp.where((trow >= tcol) | (tcol >= ATT_KEYS), 1.0, 0.0).astype(BF16)

    for qb in range(ATT_QBLOCKS):
        q = q_ref[qb * ATT_KEYS:(qb + 1) * ATT_KEYS, :]
        zero = jnp.zeros_like(q)
        qm_ref[qb] = jnp.concatenate([jnp.where(head0, q, zero), jnp.where(head0, zero, q)], axis=0)

    half = ATT_KEYS // 2
    early = [slice(hh * ATT_KEYS, hh * ATT_KEYS + half) for hh in range(HEADS_PER_STEP)]
    late = [slice(hh * ATT_KEYS + half, (hh + 1) * ATT_KEYS) for hh in range(HEADS_PER_STEP)]

    def pick(ref, qb, parts):
        return ref[qb] if parts is None else jnp.concatenate([ref[qb, p, :] for p in parts], axis=0)

    def step(kv_blocks, keep, state, parts=None):
        nrows = ATT_PAIR_ROWS if parts is None else ATT_PAIR_ROWS // 2
        starts = [pl.multiple_of(kb * ATT_KEYS, ATT_KEYS) for kb in kv_blocks]
        zs = [_dot(pick(qm_ref, qb, parts), kt_ref[:, pl.ds(starts[qb], ATT_KEYS)]) for qb in range(ATT_QBLOCKS)]
        sps = []
        for qb in range(ATT_QBLOCKS):
            sp = jnp.maximum(zs[qb], 0.0) + jnp.log2(1.0 + jnp.exp2(_neg_abs(zs[qb])))
            sps.append(sp if keep is None else jnp.where(keep[qb], sp, 0.0))
        sums = _dot(jnp.concatenate(sps, axis=0).astype(BF16), tri_ones)
        out = []
        for qb in range(ATT_QBLOCKS):
            rows = slice(qb * nrows, (qb + 1) * nrows)
            logw = zs[qb] - sums[rows, :ATT_KEYS]
            total = sums[rows, ATT_KEYS:]
            if state is not None:
                acc, log_rest = state[qb]
                logw = logw + log_rest
            w = jnp.exp2(logw)
            if keep is not None:
                w = jnp.where(keep[qb], w, 0.0)
            pv = _dot(w.astype(BF16), v_ref[pl.ds(starts[qb], ATT_KEYS), :])
            out.append((pv, -total) if state is None else (acc + pv, log_rest - total))
        return out

    def load_state(parts=None):
        return [(pick(acc_ref, qb, parts), pick(lr_ref, qb, parts)) for qb in range(ATT_QBLOCKS)]

    def store_state(state):
        top = None
        for qb, (acc, log_rest) in enumerate(state):
            acc_ref[qb] = acc
            lr_ref[qb] = log_rest
            top = log_rest if top is None else jnp.maximum(top, log_rest)
        top_early = jnp.max(jnp.maximum(top[early[0]], top[early[1]]))
        top_late = jnp.max(jnp.maximum(top[late[0]], top[late[1]]))
        return top_early, top_late

    def store_early(state):
        top = None
        for qb, (acc, log_rest) in enumerate(state):
            for n, p in enumerate(early):
                acc_ref[qb, p, :] = acc[n * half:(n + 1) * half]
                lr_ref[qb, p, :] = log_rest[n * half:(n + 1) * half]
            top = log_rest if top is None else jnp.maximum(top, log_rest)
        return jnp.max(top)

    first_block = tile * ATT_QBLOCKS
    state = step([first_block + qb for qb in range(ATT_QBLOCKS)], [causal] * ATT_QBLOCKS, None)
    before = [first_block + qb - 1 for qb in range(ATT_QBLOCKS)]
    tops = store_state(step([jnp.maximum(kb, 0) for kb in before], [kb >= 0 for kb in before], state))

    def walk_full(carry):
        s = carry[0]
        return (s + 1,) + store_state(step([first_block + qb - s for qb in range(ATT_QBLOCKS)], None, load_state()))

    def walk_early(carry):
        s, _, top_late = carry
        state = step([first_block + qb - s for qb in range(ATT_QBLOCKS)], None, load_state(early), early)
        return s + 1, store_early(state), top_late

    def walk_ragged(carry):
        s = carry[0]
        blocks = [first_block + qb - s for qb in range(ATT_QBLOCKS)]
        state = step([jnp.maximum(kb, 0) for kb in blocks], [kb >= 0 for kb in blocks], load_state())
        return (s + 1,) + store_state(state)

    carry = (jnp.int32(2),) + tops
    carry = lax.while_loop(lambda c: (c[0] <= first_block) & (c[2] > ATT_LOG2_CUTOFF), walk_full, carry)
    carry = lax.while_loop(lambda c: (c[0] <= first_block) & (c[1] > ATT_LOG2_CUTOFF), walk_early, carry)
    lax.while_loop(lambda c: (c[0] < first_block + ATT_QBLOCKS) & (jnp.maximum(c[1], c[2]) > ATT_LOG2_CUTOFF),
                   walk_ragged, carry)

    for qb in range(ATT_QBLOCKS):
        acc = acc_ref[qb]
        out = jnp.where(head0, acc[:ATT_KEYS], acc[ATT_KEYS:])
        o_ref[qb * ATT_KEYS:(qb + 1) * ATT_KEYS, :] = out.astype(BF16)


def _attention(q, kt, v):
    seq = q.shape[0]
    assert HEADS_PER_STEP == 2
    state = pltpu.VMEM((ATT_QBLOCKS, ATT_PAIR_ROWS, ATT_KEYS), F32)
    stacked_q = pltpu.VMEM((ATT_QBLOCKS, ATT_PAIR_ROWS, V7X_LANES), BF16)
    return pl.pallas_call(
        _attn_kernel,
        name="stickbreaking_attention",
        grid=(N_HEADS // HEADS_PER_STEP, seq // ATT_ROWS),
        in_specs=[pl.BlockSpec((ATT_ROWS, V7X_LANES), lambda p, i: (i, p)),
                  pl.BlockSpec((V7X_LANES, seq), lambda p, i: (p, 0)),
                  pl.BlockSpec((seq, V7X_LANES), lambda p, i: (0, p))],
        out_specs=pl.BlockSpec((ATT_ROWS, V7X_LANES), lambda p, i: (i, p)),
        out_shape=jax.ShapeDtypeStruct((seq, D_MODEL), BF16),
        scratch_shapes=[stacked_q, state, state],
        compiler_params=pltpu.CompilerParams(
            dimension_semantics=("arbitrary", "arbitrary"), vmem_limit_bytes=_vmem_limit(40 * 1024 * 1024)),
    )(q, kt, v)


def kernel(x, norm_mix_pre, norm_mix_post, norm_ffn_pre, norm_ffn_post, w_ffn_in, w_ffn_out, s5_log_dt, s5_a_re, s5_a_im, s5_b_re, s5_b_im, s5_c_re, s5_c_im, s5_d, s5_w_glu, s5_b_glu, kv_norm, w_k, w_v, w_q, w_o):
    bsz, seq, _ = x.shape
    assert bsz == 1 and seq % S5_ROWS == 0 and seq % ATT_ROWS == 0 and seq % FFN_ROWS == 0
    blocks = seq // S5_ROWS
    h = x.reshape(blocks, V7X_SUBLANES, S5_SEG, D_MODEL).transpose(0, 2, 1, 3).reshape(seq, D_MODEL)
    s5_params = jax.vmap(_s5_params)(s5_log_dt, s5_a_re, s5_a_im, s5_b_re, s5_b_im, s5_c_re, s5_c_im)
    for layer in range(N_A_LAYERS):
        params = tuple(p[layer] for p in s5_params)
        h = _s5_layer(h, norm_mix_pre[layer], norm_mix_post[layer], params, s5_d[layer],
                      s5_w_glu[layer], s5_b_glu[layer])
        h = _ffn_layer(h, norm_ffn_pre[layer], norm_ffn_post[layer], w_ffn_in[layer], w_ffn_out[layer])
    h = h.reshape(blocks, S5_SEG, V7X_SUBLANES, D_MODEL).transpose(0, 2, 1, 3).reshape(seq, D_MODEL)

    q, kt, v = _qkv_proj(h, norm_mix_pre[N_A_LAYERS], w_q[0], kv_norm, w_k, w_v)
    for layer in range(N_A_LAYERS, DEPTH):
        b = layer - N_A_LAYERS
        att = _attention(q, kt, v)
        next_q = (norm_mix_pre[layer + 1], w_q[b + 1]) if layer + 1 < DEPTH else None
        out = _ffn_layer(h, norm_ffn_pre[layer], norm_ffn_post[layer], w_ffn_in[layer], w_ffn_out[layer],
                         att=att, w_o=w_o[b], g_mix=norm_mix_post[layer], next_q=next_q)
        h, q = out if next_q is not None else (out, None)
    return h.reshape(bsz, seq, D_MODEL)
```

```python
import functools
import math

import jax
import jax.numpy as jnp
from jax import lax
from jax.experimental import pallas as pl
from jax.experimental.pallas import tpu as pltpu

F32 = jnp.float32
BF16 = jnp.bfloat16

D_MODEL = 1024
DEPTH = 4
N_A_LAYERS = DEPTH // 2
SSM_GROUP = 16
SSM_GROUPS = D_MODEL // SSM_GROUP
SSM_STATE = 64
N_HEADS = 16
HEAD_DIM = D_MODEL // N_HEADS
D_FF = 2816
EPS = 1e-6

V7X_LANES = 128
V7X_SUBLANES = 8
BF16_ROWS = 2 * V7X_SUBLANES
V7X_MXU_DIM = 256
V7X_VMEM_BYTES = 64 * 1024 * 1024

S5_KTILES = D_MODEL // V7X_MXU_DIM
S5_TILE_GROUPS = V7X_MXU_DIM // SSM_GROUP
S5_TILE_STATES = S5_TILE_GROUPS * SSM_STATE
S5_SEG = 64
S5_ROWS = V7X_SUBLANES * S5_SEG
S5_COLS = 512

FFN_ROWS = 512
PROJ_ROWS = 512

ATT_KEYS = 128
ATT_QBLOCKS = 8
ATT_ROWS = ATT_QBLOCKS * ATT_KEYS
HEADS_PER_STEP = V7X_LANES // HEAD_DIM
ATT_PAIR_ROWS = HEADS_PER_STEP * ATT_KEYS
LOG2_E = math.log2(math.e)
ATT_LOG2_CUTOFF = -104.0 * LOG2_E


def _vmem_limit(nbytes):
    return int(min(nbytes, V7X_VMEM_BYTES - 6 * 1024 * 1024))


def _rms(x, g):
    return x * lax.rsqrt(jnp.mean(x * x, axis=-1, keepdims=True) + EPS) * g


def _dot(a, b):
    return jnp.dot(a, b, preferred_element_type=F32)


def _neg_abs(x):
    return -jnp.abs(x)


def _const_spec(shape):
    zeros = (0,) * len(shape)
    return pl.BlockSpec(shape, lambda *_: zeros, pipeline_mode=pl.Buffered(1))


def _s5_kernel(h_ref, gpre_ref, gpost_ref, wb_ref, wc_ref, lre_ref, lim_ref, mre_ref, mim_ref,
               d_ref, wglu_ref, bglu_ref, o_ref,
               xre_ref, xim_ref, xb_ref, cre_ref, cim_ref, y_ref):
    @pl.when(pl.program_id(0) == 0)
    def _():
        cre_ref[...] = jnp.zeros_like(cre_ref)
        cim_ref[...] = jnp.zeros_like(cim_ref)

    h = h_ref[...]
    u = _rms(h, gpre_ref[...])
    u_bf = u.astype(BF16)
    zeros = jnp.zeros((V7X_SUBLANES, S5_COLS), F32)
    row = lax.broadcasted_iota(jnp.int32, (V7X_SUBLANES, S5_COLS), 0)

    for kt in range(S5_KTILES):
        slot = kt % 2
        bu = _dot(u_bf[:, kt * V7X_MXU_DIM:(kt + 1) * V7X_MXU_DIM], wb_ref[kt])
        xre_ref[slot] = bu[:, :S5_TILE_STATES]
        xim_ref[slot] = bu[:, S5_TILE_STATES:]

        for cg in range(S5_TILE_STATES // S5_COLS):
            cs = slice(cg * S5_COLS, (cg + 1) * S5_COLS)
            lre = jnp.broadcast_to(lre_ref[kt, :, cs], (V7X_SUBLANES, S5_COLS))
            lim = jnp.broadcast_to(lim_ref[kt, :, cs], (V7X_SUBLANES, S5_COLS))

            er, ei = zeros, zeros
            for i in range(S5_SEG):
                rows = pl.ds(i * V7X_SUBLANES, V7X_SUBLANES)
                er, ei = (lre * er - lim * ei + xre_ref[slot, rows, cs],
                          lre * ei + lim * er + xim_ref[slot, rows, cs])
                xre_ref[slot, rows, cs] = er
                xim_ref[slot, rows, cs] = ei

            mre = mre_ref[kt, :, cs]
            mim = mim_ref[kt, :, cs]
            cur_r = cre_ref[kt, :, cs]
            cur_i = cim_ref[kt, :, cs]
            xin_r = zeros
            xin_i = zeros
            for r in range(V7X_SUBLANES):
                xin_r = jnp.where(row == r, cur_r, xin_r)
                xin_i = jnp.where(row == r, cur_i, xin_i)
                nxt_r = mre * cur_r - mim * cur_i + er[r:r + 1, :]
                nxt_i = mre * cur_i + mim * cur_r + ei[r:r + 1, :]
                cur_r, cur_i = nxt_r, nxt_i
            cre_ref[kt, :, cs] = cur_r
            cim_ref[kt, :, cs] = cur_i

            cs_im = slice(S5_TILE_STATES + cg * S5_COLS, S5_TILE_STATES + (cg + 1) * S5_COLS)

            cr, ci = xin_r, xin_i
            for j in range(S5_ROWS // BF16_ROWS):
                full_r, full_i = [], []
                for half in range(BF16_ROWS // V7X_SUBLANES):
                    cr, ci = lre * cr - lim * ci, lre * ci + lim * cr
                    rows = pl.ds(j * BF16_ROWS + half * V7X_SUBLANES, V7X_SUBLANES)
                    full_r.append(xre_ref[slot, rows, cs] + cr)
                    full_i.append(xim_ref[slot, rows, cs] + ci)
                packed = pl.ds(j * BF16_ROWS, BF16_ROWS)
                xb_ref[slot, packed, cs] = jnp.concatenate(full_r, axis=0).astype(BF16)
                xb_ref[slot, packed, cs_im] = jnp.concatenate(full_i, axis=0).astype(BF16)

        y_ref[:, kt * V7X_MXU_DIM:(kt + 1) * V7X_MXU_DIM] = _dot(xb_ref[slot], wc_ref[kt])

    y = y_ref[...] + d_ref[...] * u
    a = jax.nn.gelu(y).astype(BF16)
    z = _dot(a, wglu_ref[...]) + bglu_ref[...]
    mix = z[:, :D_MODEL] * jax.nn.sigmoid(z[:, D_MODEL:])
    o_ref[...] = h + _rms(mix, gpost_ref[...])


def _s5_params(log_dt, a_re, a_im, b_re, b_im, c_re, c_im):
    dt = jnp.exp(log_dt.astype(F32))[:, None]
    a_re = a_re.astype(F32)
    a_im = a_im.astype(F32)
    mag = jnp.exp(a_re * dt)
    lam_re = mag * jnp.cos(a_im * dt)
    lam_im = mag * jnp.sin(a_im * dt)
    den = a_re * a_re + a_im * a_im
    nr = lam_re - 1.0
    coef_re = (nr * a_re + lam_im * a_im) / den
    coef_im = (lam_im * a_re - nr * a_im) / den
    bb_re = coef_re[..., None] * b_re - coef_im[..., None] * b_im
    bb_im = coef_re[..., None] * b_im + coef_im[..., None] * b_re

    eye = jnp.eye(S5_TILE_GROUPS, dtype=F32)

    def in_proj(b):
        b = b.reshape(S5_KTILES, S5_TILE_GROUPS, SSM_STATE, SSM_GROUP)
        w = jnp.einsum('kgpc,gh->kgchp', b, eye)
        return w.reshape(S5_KTILES, V7X_MXU_DIM, S5_TILE_STATES)

    def out_proj(c):
        c = c.astype(F32).reshape(S5_KTILES, S5_TILE_GROUPS, SSM_GROUP, SSM_STATE)
        w = jnp.einsum('kgcp,gh->kgphc', c, eye)
        return w.reshape(S5_KTILES, S5_TILE_STATES, V7X_MXU_DIM)

    wb = jnp.concatenate([in_proj(bb_re), in_proj(bb_im)], axis=2).astype(BF16)
    wc = jnp.concatenate([out_proj(c_re), -out_proj(c_im)], axis=1).astype(BF16)

    def tile(v):
        return v.reshape(S5_KTILES, 1, S5_TILE_STATES)

    seg_mag = jnp.exp(float(S5_SEG) * a_re * dt)
    m_re = seg_mag * jnp.cos(float(S5_SEG) * a_im * dt)
    m_im = seg_mag * jnp.sin(float(S5_SEG) * a_im * dt)
    return wb, wc, tile(lam_re), tile(lam_im), tile(m_re), tile(m_im)


def _s5_layer(h, g_pre, g_post, params, d_skip, w_glu, b_glu):
    seq = h.shape[0]
    row_spec = pl.BlockSpec((S5_ROWS, D_MODEL), lambda i: (i, 0))
    args = (h, g_pre.reshape(1, D_MODEL), g_post.reshape(1, D_MODEL),
            *params, d_skip.reshape(1, D_MODEL).astype(F32), w_glu.astype(BF16),
            b_glu.reshape(1, 2 * D_MODEL).astype(F32))
    in_specs = [row_spec] + [_const_spec(a.shape) for a in args[1:]]
    return pl.pallas_call(
        _s5_kernel,
        name="s5_mixer",
        grid=(seq // S5_ROWS,),
        in_specs=in_specs,
        out_specs=row_spec,
        out_shape=jax.ShapeDtypeStruct((seq, D_MODEL), F32),
        scratch_shapes=[
            pltpu.VMEM((2, S5_ROWS, S5_TILE_STATES), F32),
            pltpu.VMEM((2, S5_ROWS, S5_TILE_STATES), F32),
            pltpu.VMEM((2, S5_ROWS, 2 * S5_TILE_STATES), BF16),
            pltpu.VMEM((S5_KTILES, 1, S5_TILE_STATES), F32),
            pltpu.VMEM((S5_KTILES, 1, S5_TILE_STATES), F32),
            pltpu.VMEM((S5_ROWS, D_MODEL), F32),
        ],
        compiler_params=pltpu.CompilerParams(
            dimension_semantics=("arbitrary",), vmem_limit_bytes=_vmem_limit(56 * 1024 * 1024)),
    )(*args)


def _ffn_kernel(*refs, has_att, has_next_q):
    refs = list(refs)
    h = refs.pop(0)[...]
    if has_att:
        att_ref, wo_ref, gmix_ref = refs[:3]
        del refs[:3]
        h = h + _rms(_dot(att_ref[...], wo_ref[...]), gmix_ref[...])
    gpre_ref, gpost_ref, win_ref, wout_ref = refs[:4]
    del refs[:4]
    if has_next_q:
        gq_ref, wq_ref = refs[:2]
        del refs[:2]
    o_ref = refs.pop(0)
    xn = _rms(h, gpre_ref[...]).astype(BF16)
    hd = _dot(xn, win_ref[...])
    a = (jax.nn.silu(hd[:, :D_FF]) * hd[:, D_FF:]).astype(BF16)
    out = h + _rms(_dot(a, wout_ref[...]), gpost_ref[...])
    o_ref[...] = out
    if has_next_q:
        q_ref = refs.pop(0)
        q_ref[...] = _q_of(out, gq_ref, wq_ref)


def _ffn_layer(h, g_pre, g_post, w_in, w_out, att=None, w_o=None, g_mix=None, next_q=None):
    seq = h.shape[0]
    row_spec = pl.BlockSpec((FFN_ROWS, D_MODEL), lambda i: (i, 0))
    args, specs = [h], [row_spec]
    if att is not None:
        args += [att, w_o.astype(BF16), g_mix.reshape(1, D_MODEL)]
        specs += [row_spec, _const_spec((D_MODEL, D_MODEL)), _const_spec((1, D_MODEL))]
    consts = [g_pre.reshape(1, D_MODEL), g_post.reshape(1, D_MODEL), w_in.astype(BF16), w_out.astype(BF16)]
    if next_q is not None:
        consts += [next_q[0].reshape(1, D_MODEL), next_q[1].astype(BF16)]
    args += consts
    specs += [_const_spec(a.shape) for a in consts]
    out_specs, out_shape = [row_spec], [jax.ShapeDtypeStruct((seq, D_MODEL), F32)]
    if next_q is not None:
        out_specs.append(row_spec)
        out_shape.append(jax.ShapeDtypeStruct((seq, D_MODEL), BF16))
    outs = pl.pallas_call(
        functools.partial(_ffn_kernel, has_att=att is not None, has_next_q=next_q is not None),
        name="ffn",
        grid=(seq // FFN_ROWS,),
        in_specs=specs,
        out_specs=out_specs,
        out_shape=out_shape,
        compiler_params=pltpu.CompilerParams(
            dimension_semantics=("arbitrary",), vmem_limit_bytes=_vmem_limit(56 * 1024 * 1024)),
    )(*args)
    return outs[0] if next_q is None else tuple(outs)


def _q_of(h, gq_ref, wq_ref):
    u = _rms(h, gq_ref[...]).astype(BF16)
    return (_dot(u, wq_ref[...]) * (HEAD_DIM ** -0.5 * LOG2_E)).astype(BF16)


def _qkv_kernel(h_ref, gq_ref, wq_ref, gkv_ref, wkt_ref, wv_ref, q_ref, kt_ref, v_ref):
    h = h_ref[...]
    q_ref[...] = _q_of(h, gq_ref, wq_ref)
    hk = _rms(h, gkv_ref[...]).astype(BF16)
    kt_ref[...] = lax.dot_general(wkt_ref[...], hk, (((1,), (1,)), ((), ())),
                                  preferred_element_type=F32).astype(BF16)
    v_ref[...] = _dot(hk, wv_ref[...]).astype(BF16)


def _qkv_proj(h, g_q, w_q, g_kv, w_k, w_v):
    seq = h.shape[0]
    row_spec = pl.BlockSpec((PROJ_ROWS, D_MODEL), lambda i: (i, 0))
    col_spec = pl.BlockSpec((D_MODEL, PROJ_ROWS), lambda i: (0, i))
    args = (h, g_q.reshape(1, D_MODEL), w_q.astype(BF16), g_kv.reshape(1, D_MODEL),
            w_k.T.astype(BF16), w_v.astype(BF16))
    return pl.pallas_call(
        _qkv_kernel,
        name="qkv_proj",
        grid=(seq // PROJ_ROWS,),
        in_specs=[row_spec] + [_const_spec(a.shape) for a in args[1:]],
        out_specs=[row_spec, col_spec, row_spec],
        out_shape=[jax.ShapeDtypeStruct((seq, D_MODEL), BF16),
                   jax.ShapeDtypeStruct((D_MODEL, seq), BF16),
                   jax.ShapeDtypeStruct((seq, D_MODEL), BF16)],
        compiler_params=pltpu.CompilerParams(
            dimension_semantics=("arbitrary",), vmem_limit_bytes=_vmem_limit(40 * 1024 * 1024)),
    )(*args)


def _attn_kernel(q_ref, kt_ref, v_ref, o_ref, qm_ref, acc_ref, lr_ref):
    tile = pl.program_id(1)
    lane = lax.broadcasted_iota(jnp.int32, (ATT_KEYS, V7X_LANES), 1)
    head0 = lane < HEAD_DIM
    srow = lax.broadcasted_iota(jnp.int32, (ATT_PAIR_ROWS, ATT_KEYS), 0)
    skey = lax.broadcasted_iota(jnp.int32, (ATT_PAIR_ROWS, ATT_KEYS), 1)
    causal = jnp.where(srow >= ATT_KEYS, srow - ATT_KEYS, srow) > skey
    trow = lax.broadcasted_iota(jnp.int32, (ATT_KEYS, 2 * ATT_KEYS), 0)
    tcol = lax.broadcasted_iota(jnp.int32, (ATT_KEYS, 2 * ATT_KEYS), 1)
    tri_ones = jnp.where((trow >= tcol) | (tcol >= ATT_KEYS), 1.0, 0.0).astype(BF16)

    for qb in range(ATT_QBLOCKS):
        q = q_ref[qb * ATT_KEYS:(qb + 1) * ATT_KEYS, :]
        zero = jnp.zeros_like(q)
        qm_ref[qb] = jnp.concatenate([jnp.where(head0, q, zero), jnp.where(head0, zero, q)], axis=0)

    half = ATT_KEYS // 4
    early = [slice(hh * ATT_KEYS, hh * ATT_KEYS + half) for hh in range(HEADS_PER_STEP)]
    late = [slice(hh * ATT_KEYS + half, (hh + 1) * ATT_KEYS) for hh in range(HEADS_PER_STEP)]

    def pick(ref, qb, parts):
        return ref[qb] if parts is None else jnp.concatenate([ref[qb, p, :] for p in parts], axis=0)

    def step(kv_blocks, keep, state, parts=None):
        nrows = ATT_PAIR_ROWS if parts is None else HEADS_PER_STEP * half
        starts = [pl.multiple_of(kb * ATT_KEYS, ATT_KEYS) for kb in kv_blocks]
        zs = [_dot(pick(qm_ref, qb, parts), kt_ref[:, pl.ds(starts[qb], ATT_KEYS)]) for qb in range(ATT_QBLOCKS)]
        sps = []
        for qb in range(ATT_QBLOCKS):
            sp = jnp.maximum(zs[qb], 0.0) + jnp.log2(1.0 + jnp.exp2(_neg_abs(zs[qb])))
            sps.append(sp if keep is None else jnp.where(keep[qb], sp, 0.0))
        sums = _dot(jnp.concatenate(sps, axis=0).astype(BF16), tri_ones)
        out = []
        for qb in range(ATT_QBLOCKS):
            rows = slice(qb * nrows, (qb + 1) * nrows)
            logw = zs[qb] - sums[rows, :ATT_KEYS]
            total = sums[rows, ATT_KEYS:]
            if state is not None:
                acc, log_rest = state[qb]
                logw = logw + log_rest
            w = jnp.exp2(logw)
            if keep is not None:
                w = jnp.where(keep[qb], w, 0.0)
            pv = _dot(w.astype(BF16), v_ref[pl.ds(starts[qb], ATT_KEYS), :])
            out.append((pv, -total) if state is None else (acc + pv, log_rest - total))
        return out

    def load_state(parts=None):
        return [(pick(acc_ref, qb, parts), pick(lr_ref, qb, parts)) for qb in range(ATT_QBLOCKS)]

    def store_state(state):
        top = None
        for qb, (acc, log_rest) in enumerate(state):
            acc_ref[qb] = acc
            lr_ref[qb] = log_rest
            top = log_rest if top is None else jnp.maximum(top, log_rest)
        top_early = jnp.max(jnp.maximum(top[early[0]], top[early[1]]))
        top_late = jnp.max(jnp.maximum(top[late[0]], top[late[1]]))
        return top_early, top_late

    def store_early(state):
        top = None
        for qb, (acc, log_rest) in enumerate(state):
            for n, p in enumerate(early):
                acc_ref[qb, p, :] = acc[n * half:(n + 1) * half]
                lr_ref[qb, p, :] = log_rest[n * half:(n + 1) * half]
            top = log_rest if top is None else jnp.maximum(top, log_rest)
        return jnp.max(top)

    first_block = tile * ATT_QBLOCKS
    state = step([first_block + qb for qb in range(ATT_QBLOCKS)], [causal] * ATT_QBLOCKS, None)
    before = [first_block + qb - 1 for qb in range(ATT_QBLOCKS)]
    tops = store_state(step([jnp.maximum(kb, 0) for kb in before], [kb >= 0 for kb in before], state))

    def walk_full(carry):
        s = carry[0]
        return (s + 1,) + store_state(step([first_block + qb - s for qb in range(ATT_QBLOCKS)], None, load_state()))

    def walk_early(carry):
        s, _, top_late = carry
        state = step([first_block + qb - s for qb in range(ATT_QBLOCKS)], None, load_state(early), early)
        return s + 1, store_early(state), top_late

    def walk_ragged(carry):
        s = carry[0]
        blocks = [first_block + qb - s for qb in range(ATT_QBLOCKS)]
        state = step([jnp.maximum(kb, 0) for kb in blocks], [kb >= 0 for kb in blocks], load_state())
        return (s + 1,) + store_state(state)

    carry = (jnp.int32(2),) + tops
    carry = lax.while_loop(lambda c: (c[0] <= first_block) & (c[2] > ATT_LOG2_CUTOFF), walk_full, carry)
    carry = lax.while_loop(lambda c: (c[0] <= first_block) & (c[1] > ATT_LOG2_CUTOFF), walk_early, carry)
    lax.while_loop(lambda c: (c[0] < first_block + ATT_QBLOCKS) & (jnp.maximum(c[1], c[2]) > ATT_LOG2_CUTOFF),
                   walk_ragged, carry)

    for qb in range(ATT_QBLOCKS):
        acc = acc_ref[qb]
        out = jnp.where(head0, acc[:ATT_KEYS], acc[ATT_KEYS:])
        o_ref[qb * ATT_KEYS:(qb + 1) * ATT_KEYS, :] = out.astype(BF16)


def _attention(q, kt, v):
    seq = q.shape[0]
    assert HEADS_PER_STEP == 2
    state = pltpu.VMEM((ATT_QBLOCKS, ATT_PAIR_ROWS, ATT_KEYS), F32)
    stacked_q = pltpu.VMEM((ATT_QBLOCKS, ATT_PAIR_ROWS, V7X_LANES), BF16)
    return pl.pallas_call(
        _attn_kernel,
        name="stickbreaking_attention",
        grid=(N_HEADS // HEADS_PER_STEP, seq // ATT_ROWS),
        in_specs=[pl.BlockSpec((ATT_ROWS, V7X_LANES), lambda p, i: (i, p)),
                  pl.BlockSpec((V7X_LANES, seq), lambda p, i: (p, 0)),
                  pl.BlockSpec((seq, V7X_LANES), lambda p, i: (0, p))],
        out_specs=pl.BlockSpec((ATT_ROWS, V7X_LANES), lambda p, i: (i, p)),
        out_shape=jax.ShapeDtypeStruct((seq, D_MODEL), BF16),
        scratch_shapes=[stacked_q, state, state],
        compiler_params=pltpu.CompilerParams(
            dimension_semantics=("arbitrary", "arbitrary"), vmem_limit_bytes=_vmem_limit(40 * 1024 * 1024)),
    )(q, kt, v)


def kernel(x, norm_mix_pre, norm_mix_post, norm_ffn_pre, norm_ffn_post, w_ffn_in, w_ffn_out, s5_log_dt, s5_a_re, s5_a_im, s5_b_re, s5_b_im, s5_c_re, s5_c_im, s5_d, s5_w_glu, s5_b_glu, kv_norm, w_k, w_v, w_q, w_o):
    bsz, seq, _ = x.shape
    assert bsz == 1 and seq % S5_ROWS == 0 and seq % ATT_ROWS == 0 and seq % FFN_ROWS == 0
    blocks = seq // S5_ROWS
    h = x.reshape(blocks, V7X_SUBLANES, S5_SEG, D_MODEL).transpose(0, 2, 1, 3).reshape(seq, D_MODEL)
    s5_params = jax.vmap(_s5_params)(s5_log_dt, s5_a_re, s5_a_im, s5_b_re, s5_b_im, s5_c_re, s5_c_im)
    for layer in range(N_A_LAYERS):
        params = tuple(p[layer] for p in s5_params)
        h = _s5_layer(h, norm_mix_pre[layer], norm_mix_post[layer], params, s5_d[layer],
                      s5_w_glu[layer], s5_b_glu[layer])
        h = _ffn_layer(h, norm_ffn_pre[layer], norm_ffn_post[layer], w_ffn_in[layer], w_ffn_out[layer])
    h = h.reshape(blocks, S5_SEG, V7X_SUBLANES, D_MODEL).transpose(0, 2, 1, 3).reshape(seq, D_MODEL)

    q, kt, v = _qkv_proj(h, norm_mix_pre[N_A_LAYERS], w_q[0], kv_norm, w_k, w_v)
    for layer in range(N_A_LAYERS, DEPTH):
        b = layer - N_A_LAYERS
        att = _attention(q, kt, v)
        next_q = (norm_mix_pre[layer + 1], w_q[b + 1]) if layer + 1 < DEPTH else None
        out = _ffn_layer(h, norm_ffn_pre[layer], norm_ffn_post[layer], w_ffn_in[layer], w_ffn_out[layer],
                         att=att, w_o=w_o[b], g_mix=norm_mix_post[layer], next_q=next_q)
        h, q = out if next_q is not None else (out, None)
    return h.reshape(bsz, seq, D_MODEL)
```
